```python
import jax, jax.numpy as jnp
from jax import lax
import numpy as np

D_MODEL = 2048
BATCH = 4
SEQ = 2048
DEPTH = 4
DEC_BATCH = 128
DEC_SEQ = 8
PAST_LEN = 16384
PAGE_SIZE = 128

MIX_WIDTH = D_MODEL
POOL_WIDTH = MIX_WIDTH // 2
POOL_WINDOWS = (2, 4, 8, 16)
POOL_GROUP = POOL_WIDTH // len(POOL_WINDOWS)
POOL_BUF = max(POOL_WINDOWS) - 1
RWKV_WIDTH = MIX_WIDTH - POOL_WIDTH
HEAD_DIM = 64
N_HEADS = RWKV_WIDTH // HEAD_DIM
DECAY_LORA = 64
AAA_LORA = 64
GATE_LORA = 160
SHIFT_WIDTH = 3 * RWKV_WIDTH + DECAY_LORA + AAA_LORA + GATE_LORA
PROJ_WIDTH = POOL_WIDTH + SHIFT_WIDTH
LNX_EPS = 64e-5
RMS_EPS = 1e-6
N_EXPERTS = 16
N_EXPERT_GROUPS = 4
EXPERTS_PER_GROUP = N_EXPERTS // N_EXPERT_GROUPS
TOP_K = 2
GROUP_SCORE_K = 2
D_EXPERT = D_MODEL // 2
MOE_BLOCK = 128

kernel_name = "hybrid_pool_rwkv7_moe_adaln_step"


def _rmsnorm(x, g):
    xf = x.astype(jnp.float32)
    return xf * lax.rsqrt(jnp.mean(xf * xf, -1, keepdims=True) + RMS_EPS) * g.astype(jnp.float32)


def _ada_norm(x, g, shift, scale):
    h = _rmsnorm(x, g) * (1.0 + scale[:, None, :].astype(jnp.float32)) + shift[:, None, :].astype(jnp.float32)
    return h.astype(x.dtype)


def _pool_mixer(p, buf, pos0, w_pool, scale):
    B, L, _ = p.shape
    ext = jnp.concatenate([buf.astype(p.dtype), p], 1)
    cs = jnp.concatenate([jnp.zeros((B, 1, POOL_WIDTH), jnp.float32),
                          jnp.cumsum(ext.astype(jnp.float32), 1)], 1)
    end = cs[:, POOL_BUF + 1:]
    pos = pos0 + jnp.arange(L)
    pf = p.astype(jnp.float32)
    outs = []
    for gi, win in enumerate(POOL_WINDOWS):
        sl = slice(gi * POOL_GROUP, (gi + 1) * POOL_GROUP)
        s0 = POOL_BUF + 1 - win
        wsum = end[..., sl] - cs[:, s0:s0 + L, sl]
        cnt = jnp.minimum(pos + 1, win).astype(jnp.float32)[None, :, None]
        pooled = wsum / cnt - pf[..., sl]
        outs.append(jnp.einsum('blc,cd->bld', pooled, w_pool[gi].astype(jnp.float32)))
    y = jnp.concatenate(outs, -1) * scale.astype(jnp.float32)
    return y.astype(p.dtype), ext[:, -POOL_BUF:]


def _rwkv7(z, shift_prev, S0, mu, w0, w2, a0, a2, g2, k_k, k_a, r_k, lnx_w, lnx_b):
    B, L, _ = z.shape
    f32 = jnp.float32
    zf = z.astype(f32)
    zprev = jnp.concatenate([shift_prev.astype(f32)[:, None], zf[:, :-1]], 1)
    zs = zf + (zprev - zf) * mu.astype(f32)
    o = 3 * RWKV_WIDTH
    r = zs[..., :RWKV_WIDTH]
    k = zs[..., RWKV_WIDTH:2 * RWKV_WIDTH]
    v = zs[..., 2 * RWKV_WIDTH:o]
    wd = zs[..., o:o + DECAY_LORA]
    ad = zs[..., o + DECAY_LORA:o + DECAY_LORA + AAA_LORA]
    gd = zs[..., o + DECAY_LORA + AAA_LORA:]
    w = -jax.nn.softplus(-(w0.astype(f32) + jnp.tanh(wd) @ w2.astype(f32))) - 0.5
    a = jax.nn.sigmoid(a0.astype(f32) + ad @ a2.astype(f32))
    g = jax.nn.sigmoid(gd) @ g2.astype(f32)
    heads = lambda t: t.reshape(B, L, N_HEADS, HEAD_DIM)
    kk = heads(k * k_k.astype(f32))
    kk = kk / jnp.maximum(jnp.sqrt(jnp.sum(kk * kk, -1, keepdims=True)), 1e-12)
    k = k * (1.0 + (a - 1.0) * k_a.astype(f32))
    decay = jnp.exp(-jnp.exp(w))
    tm = lambda t: jnp.moveaxis(t, 1, 0)
    seqs = (tm(heads(r)), tm(heads(decay)), tm(heads(k)), tm(heads(v)), tm(-kk), tm(kk * heads(a)))

    def step(S, inp):
        rt, wt, kt, vt, at, bt = inp
        sa = jnp.einsum('bhij,bhj->bhi', S, at)
        S = S * wt[:, :, None, :] + sa[..., None] * bt[:, :, None, :] + vt[..., None] * kt[:, :, None, :]
        return S, jnp.einsum('bhij,bhj->bhi', S, rt)

    S_fin, ys = lax.scan(step, S0.astype(f32), seqs)
    y = jnp.moveaxis(ys, 0, 1)
    mean = jnp.mean(y, -1, keepdims=True)
    var = jnp.mean(jnp.square(y - mean), -1, keepdims=True)
    y = ((y - mean) * lax.rsqrt(var + LNX_EPS)).reshape(B, L, RWKV_WIDTH) * lnx_w.astype(f32) + lnx_b.astype(f32)
    bonus = jnp.sum(heads(r * k * r_k.astype(f32)), -1, keepdims=True) * heads(v)
    y = (y + bonus.reshape(B, L, RWKV_WIDTH)) * g
    return y.astype(z.dtype), S_fin, z[:, -1]


def _moe(h, w_router, b_router, w_gate, w_up, w_down):
    T, D = h.shape
    f32 = jnp.float32
    scores = jax.nn.sigmoid(jnp.einsum('td,de->te', h, w_router).astype(f32))
    sel = scores + b_router.astype(f32)
    gscore = jnp.sum(lax.top_k(sel.reshape(T, N_EXPERT_GROUPS, EXPERTS_PER_GROUP), GROUP_SCORE_K)[0], -1)
    gbest = jnp.argmax(gscore, -1)
    in_group = (jnp.arange(N_EXPERTS) // EXPERTS_PER_GROUP)[None, :] == gbest[:, None]
    _, idx = lax.top_k(jnp.where(in_group, sel, -jnp.inf), TOP_K)
    wts = jnp.take_along_axis(scores, idx, 1)
    wts = wts / jnp.sum(wts, -1, keepdims=True)
    A = T * TOP_K
    e_flat = idx.reshape(A)
    tok_flat = jnp.repeat(jnp.arange(T, dtype=jnp.int32), TOP_K)
    w_flat = wts.reshape(A)
    order = jnp.argsort(e_flat, stable=True)
    e_sorted = e_flat[order]
    counts = jnp.bincount(e_flat, length=N_EXPERTS)
    start = jnp.cumsum(counts) - counts
    padded = (counts + MOE_BLOCK - 1) // MOE_BLOCK * MOE_BLOCK
    pend = jnp.cumsum(padded)
    pstart = pend - padded
    dest = pstart[e_sorted] + (jnp.arange(A) - start[e_sorted])
    n_blocks = -(-(A + N_EXPERTS * (MOE_BLOCK - 1)) // MOE_BLOCK)
    R = n_blocks * MOE_BLOCK
    row_tok = jnp.zeros((R,), jnp.int32).at[dest].set(tok_flat[order])
    row_w = jnp.zeros((R,), f32).at[dest].set(w_flat[order])
    block_e = jnp.searchsorted(pend, jnp.arange(n_blocks) * MOE_BLOCK, side='right')
    block_e = jnp.minimum(block_e, N_EXPERTS - 1)

    def run_block(args):
        toks, e = args
        xb = h[toks]
        act = jax.nn.silu(xb @ w_gate[e]) * (xb @ w_up[e])
        return act @ w_down[e]

    yb = lax.map(run_block, (row_tok.reshape(n_blocks, MOE_BLOCK), block_e))
    y = yb.reshape(R, D) * row_w[:, None].astype(yb.dtype)
    return jnp.zeros_like(h).at[row_tok].add(y.astype(h.dtype))


def _trunk(x, c, st_wkv, st_pool, st_shift, pos0, prm):
    B, L, D = x.shape
    new_wkv, new_pool, new_shift = [], [], []
    for l in range(DEPTH):
        mod = jnp.einsum('bd,dk->bk', jax.nn.silu(c), prm['w_ada'][l]) + prm['b_ada'][l]
        sh_a, sc_a, gt_a, sh_m, sc_m, gt_m = jnp.split(mod, 6, axis=-1)
        h = _ada_norm(x, prm['norm1_g'][l], sh_a, sc_a)
        proj = jnp.einsum('bld,dp->blp', h, prm['w_in'][l])
        y_pool, buf = _pool_mixer(proj[..., :POOL_WIDTH], st_pool[l], pos0, prm['w_pool'][l], prm['pool_scale'][l])
        y_rwkv, S_fin, sh = _rwkv7(proj[..., POOL_WIDTH:], st_shift[l], st_wkv[l], prm['mu_shift'][l],
                                   prm['w0'][l], prm['w2'][l], prm['a0'][l], prm['a2'][l], prm['g2'][l],
                                   prm['k_k'][l], prm['k_a'][l], prm['r_k'][l], prm['lnx_w'][l], prm['lnx_b'][l])
        mix = jnp.einsum('blm,md->bld', jnp.concatenate([y_pool, y_rwkv], -1), prm['w_out'][l])
        x = x + (gt_a[:, None, :] * mix).astype(x.dtype)
        h2 = _ada_norm(x, prm['norm2_g'][l], sh_m, sc_m)
        ff = _moe(h2.reshape(B * L, D), prm['w_router'], prm['b_router'],
                  prm['w_gate'][l], prm['w_up'][l], prm['w_down'][l]).reshape(B, L, D)
        x = x + (gt_m[:, None, :] * ff).astype(x.dtype)
        new_wkv.append(S_fin.astype(x.dtype))
        new_pool.append(buf.astype(x.dtype))
        new_shift.append(sh.astype(x.dtype))
    y = _rmsnorm(x, prm['final_g']).astype(x.dtype)
    return y, jnp.stack(new_wkv), jnp.stack(new_pool), jnp.stack(new_shift)


def setup_inputs(seed: int = 0) -> dict:
    key = jax.random.key(seed)
    k = jax.random.split(key, 32)
    f32 = jnp.float32
    D = D_MODEL
    nrm = lambda kk, shape, s: jax.random.normal(kk, shape, f32) * s
    return {
        'x_prompt': nrm(k[0], (BATCH, SEQ, D), 1.0),
        'x_sample': nrm(k[1], (DEC_BATCH, DEC_SEQ, D), 1.0),
        'state_wkv': nrm(k[2], (DEPTH, DEC_BATCH, N_HEADS, HEAD_DIM, HEAD_DIM), 0.3),
        'state_pool': nrm(k[3], (DEPTH, DEC_BATCH, POOL_BUF, POOL_WIDTH), 1.0),
        'state_shift': nrm(k[4], (DEPTH, DEC_BATCH, SHIFT_WIDTH), 1.0),
        'c_prompt': nrm(k[5], (BATCH, D), 1.0),
        'c_sample': nrm(k[6], (DEC_BATCH, D), 1.0),
        'w_ada': nrm(k[7], (DEPTH, D, 6 * D), 0.5 * D ** -0.5),
        'b_ada': nrm(k[8], (DEPTH, 6 * D), 0.02),
        'norm1_g': 1.0 + nrm(k[9], (DEPTH, D), 0.02),
        'norm2_g': 1.0 + nrm(k[10], (DEPTH, D), 0.02),
        'w_in': nrm(k[11], (DEPTH, D, PROJ_WIDTH), D ** -0.5),
        'w_pool': nrm(k[12], (DEPTH, len(POOL_WINDOWS), POOL_GROUP, POOL_GROUP), POOL_GROUP ** -0.5),
        'pool_scale': 1.0 + nrm(k[13], (DEPTH, POOL_WIDTH), 0.1),
        'mu_shift': jax.random.uniform(k[14], (DEPTH, SHIFT_WIDTH), f32, 0.0, 1.0),
        'w0': jax.random.uniform(k[15], (DEPTH, RWKV_WIDTH), f32, -4.0, 0.0),
        'w2': nrm(k[16], (DEPTH, DECAY_LORA, RWKV_WIDTH), 0.1 * DECAY_LORA ** -0.5),
        'a0': nrm(k[17], (DEPTH, RWKV_WIDTH), 0.1),
        'a2': nrm(k[18], (DEPTH, AAA_LORA, RWKV_WIDTH), 0.1 * AAA_LORA ** -0.5),
        'g2': nrm(k[19], (DEPTH, GATE_LORA, RWKV_WIDTH), GATE_LORA ** -0.5),
        'k_k': 0.85 + nrm(k[20], (DEPTH, RWKV_WIDTH), 0.05),
        'k_a': 1.0 + nrm(k[21], (DEPTH, RWKV_WIDTH), 0.05),
        'r_k': nrm(k[22], (DEPTH, RWKV_WIDTH), 0.1),
        'lnx_w': 1.0 + nrm(k[23], (DEPTH, RWKV_WIDTH), 0.02),
        'lnx_b': nrm(k[24], (DEPTH, RWKV_WIDTH), 0.02),
        'w_out': nrm(k[25], (DEPTH, MIX_WIDTH, D), MIX_WIDTH ** -0.5),
        'w_router': nrm(k[26], (D, N_EXPERTS), D ** -0.5),
        'b_router': nrm(k[27], (N_EXPERTS,), 0.01),
        'w_gate': nrm(k[28], (DEPTH, N_EXPERTS, D, D_EXPERT), D ** -0.5),
        'w_up': nrm(k[29], (DEPTH, N_EXPERTS, D, D_EXPERT), D ** -0.5),
        'w_down': nrm(k[30], (DEPTH, N_EXPERTS, D_EXPERT, D), D_EXPERT ** -0.5),
        'final_g': 1.0 + nrm(k[31], (D,), 0.02),
    }


def reference(x_prompt, x_sample, state_wkv, state_pool, state_shift, c_prompt, c_sample,
              w_ada, b_ada, norm1_g, norm2_g, w_in, w_pool, pool_scale, mu_shift,
              w0, w2, a0, a2, g2, k_k, k_a, r_k, lnx_w, lnx_b, w_out,
              w_router, b_router, w_gate, w_up, w_down, final_g):
    prm = dict(w_ada=w_ada, b_ada=b_ada, norm1_g=norm1_g, norm2_g=norm2_g, w_in=w_in, w_pool=w_pool,
               pool_scale=pool_scale, mu_shift=mu_shift, w0=w0, w2=w2, a0=a0, a2=a2, g2=g2,
               k_k=k_k, k_a=k_a, r_k=r_k, lnx_w=lnx_w, lnx_b=lnx_b, w_out=w_out,
               w_router=w_router, b_router=b_router, w_gate=w_gate, w_up=w_up, w_down=w_down,
               final_g=final_g)
    B = x_prompt.shape[0]
    dt = x_prompt.dtype
    z_wkv = jnp.zeros((DEPTH, B, N_HEADS, HEAD_DIM, HEAD_DIM), dt)
    z_pool = jnp.zeros((DEPTH, B, POOL_BUF, POOL_WIDTH), dt)
    z_shift = jnp.zeros((DEPTH, B, SHIFT_WIDTH), dt)
    y_prompt, wkv_p, pool_p, shift_p = _trunk(x_prompt, c_prompt, z_wkv, z_pool, z_shift, 0, prm)
    y_sample, wkv_s, pool_s, shift_s = _trunk(x_sample, c_sample, state_wkv, state_pool, state_shift, PAST_LEN, prm)
    return (y_prompt, y_sample, wkv_p, pool_p, shift_p, wkv_s, pool_s, shift_s)
```

```python
import functools

import numpy as np
import jax
import jax.numpy as jnp
from jax import lax
from jax.experimental import pallas as pl
from jax.experimental.pallas import tpu as pltpu

F32 = jnp.float32
BF16 = jnp.bfloat16

D_MODEL = 2048
DEPTH = 4
PAST_LEN = 16384
POOL_WIDTH = 1024
POOL_WINDOWS = (2, 4, 8, 16)
POOL_GROUP = 256
POOL_BUF = 15
POOL_HALO = 16
RWKV_WIDTH = 1024
HEAD_DIM = 64
N_HEADS = 16
DECAY_LORA = 64
AAA_LORA = 64
GATE_LORA = 160
SHIFT_WIDTH = 3 * RWKV_WIDTH + DECAY_LORA + AAA_LORA + GATE_LORA
Z_WIDTH = 3456
LORA_OFF = 3 * RWKV_WIDTH
GATE_OFF = LORA_OFF + 128
LNX_EPS = 64e-5
RMS_EPS = 1e-6
N_EXPERTS = 16
N_EXPERT_GROUPS = 4
EXPERTS_PER_GROUP = 4
TOP_K = 2
D_EXPERT = 1024

LANES = 128
SUBLANES = 8
ROW_TILE = 256
MOE_BLOCK = 256
VMEM_LIMIT = 56 * 1024 * 1024
HIGHEST = lax.Precision.HIGHEST

_P = np.arange(RWKV_WIDTH)
VH_PERM = (_P % N_HEADS) * HEAD_DIM + _P // N_HEADS
_C = np.arange(RWKV_WIDTH)
VH_INV = (_C % HEAD_DIM) * N_HEADS + _C // HEAD_DIM


def _params(*sem):
    return pltpu.CompilerParams(dimension_semantics=sem, vmem_limit_bytes=VMEM_LIMIT)


def _sigmoid(x):
    return 1.0 / (1.0 + jnp.exp(-x))


def _tiles(B, L):
    if L >= ROW_TILE:
        assert L % ROW_TILE == 0
        return 1, ROW_TILE
    assert ROW_TILE % L == 0 and L % SUBLANES == 0 and B % (ROW_TILE // L) == 0
    return ROW_TILE // L, L


def _ada_kernel(c_ref, w_ref, b_ref, o_ref):
    c = c_ref[...]
    s = (c * _sigmoid(c)).astype(BF16)
    o_ref[0] = jnp.dot(s, w_ref[0].astype(BF16), preferred_element_type=F32) + b_ref[0]


def _ada(c_all, w_ada, b_ada):
    bc = c_all.shape[0]
    tn = 512
    return pl.pallas_call(
        _ada_kernel,
        grid=(DEPTH, 6 * D_MODEL // tn),
        in_specs=[pl.BlockSpec((bc, D_MODEL), lambda l, n: (0, 0)),
                  pl.BlockSpec((1, D_MODEL, tn), lambda l, n: (l, 0, n)),
                  pl.BlockSpec((1, 1, tn), lambda l, n: (l, 0, n))],
        out_specs=pl.BlockSpec((1, bc, tn), lambda l, n: (l, 0, n)),
        out_shape=jax.ShapeDtypeStruct((DEPTH, bc, 6 * D_MODEL), F32),
        compiler_params=_params("arbitrary", "arbitrary"),
        name="ada_mod",
    )(c_all, w_ada, b_ada.reshape(DEPTH, 1, 6 * D_MODEL))


def _ada_norm_val(x, g, shift, scale):
    ms = jnp.mean(x * x, axis=-1, keepdims=True)
    return x * lax.rsqrt(ms + RMS_EPS) * g * (1.0 + scale) + shift


def _inproj_kernel(x_ref, g_ref, sh_ref, sc_ref, wp_ref, wz_ref, p_ref, z_ref):
    bb, tl, _ = x_ref.shape
    h = _ada_norm_val(x_ref[...], g_ref[...], sh_ref[...], sc_ref[...])
    h = h.reshape(bb * tl, D_MODEL).astype(BF16)
    p_ref[...] = jnp.dot(h, wp_ref[...], preferred_element_type=F32).reshape(bb, tl, POOL_WIDTH)
    step = 1152
    for n0 in range(0, Z_WIDTH, step):
        z_ref[:, :, n0:n0 + step] = jnp.dot(
            h, wz_ref[:, n0:n0 + step], preferred_element_type=F32).reshape(bb, tl, step)


def _inproj(x, g, sh, sc, wp, wz):
    B, L, _ = x.shape
    bb, tl = _tiles(B, L)
    once = pl.Buffered(1)
    return pl.pallas_call(
        _inproj_kernel,
        grid=(B // bb, L // tl),
        in_specs=[pl.BlockSpec((bb, tl, D_MODEL), lambda b, i: (b, i, 0)),
                  pl.BlockSpec((1, D_MODEL), lambda b, i: (0, 0)),
                  pl.BlockSpec((bb, 1, D_MODEL), lambda b, i: (b, 0, 0)),
                  pl.BlockSpec((bb, 1, D_MODEL), lambda b, i: (b, 0, 0)),
                  pl.BlockSpec((D_MODEL, POOL_WIDTH), lambda b, i: (0, 0), pipeline_mode=once),
                  pl.BlockSpec((D_MODEL, Z_WIDTH), lambda b, i: (0, 0), pipeline_mode=once)],
        out_specs=[pl.BlockSpec((bb, tl, POOL_WIDTH), lambda b, i: (b, i, 0)),
                   pl.BlockSpec((bb, tl, Z_WIDTH), lambda b, i: (b, i, 0))],
        out_shape=[jax.ShapeDtypeStruct((B, L, POOL_WIDTH), F32),
                   jax.ShapeDtypeStruct((B, L, Z_WIDTH), F32)],
        compiler_params=_params("arbitrary", "arbitrary"),
        name="norm_inproj",
    )(x, g, sh, sc, wp, wz)


def _pool_kernel(pos0, p_ref, halo_ref, st_ref, w_ref, scale_ref, y_ref, buf_ref, ext_ref):
    bb, tl, _ = p_ref.shape
    i = pl.program_id(1)

    @pl.when(i == 0)
    def _():
        ext_ref[:, 0:POOL_HALO, :] = st_ref[...]

    @pl.when(i > 0)
    def _():
        ext_ref[:, 0:POOL_HALO, :] = halo_ref[...]

    ext_ref[:, POOL_HALO:POOL_HALO + tl, :] = p_ref[...]
    pos = pos0 + i * tl + lax.broadcasted_iota(jnp.int32, (1, tl, POOL_GROUP), 1)
    for gi, win in enumerate(POOL_WINDOWS):
        c0 = gi * POOL_GROUP
        cur = p_ref[:, :, c0:c0 + POOL_GROUP]
        acc = cur
        for s in range(1, win):
            acc = acc + ext_ref[:, POOL_HALO - s:POOL_HALO - s + tl, c0:c0 + POOL_GROUP]
        cnt = jnp.minimum(pos + 1, win).astype(F32)
        pooled = (acc / cnt - cur).reshape(bb * tl, POOL_GROUP).astype(BF16)
        out = jnp.dot(pooled, w_ref[gi], preferred_element_type=F32) * scale_ref[:, c0:c0 + POOL_GROUP]
        y_ref[:, c0:c0 + POOL_GROUP] = out.astype(BF16)

    @pl.when(i == pl.num_programs(1) - 1)
    def _():
        buf_ref[...] = ext_ref[:, tl + 1:tl + POOL_HALO, :]


def _pool(p, st16, w_pool, scale, pos0):
    B, L, _ = p.shape
    bb, tl = _tiles(B, L)
    nl = L // tl
    if L >= POOL_HALO:
        halo, hstep = p, tl // POOL_HALO
        halo_spec = pl.BlockSpec((bb, POOL_HALO, POOL_WIDTH),
                                 lambda b, i: (b, jnp.maximum(i * hstep - 1, 0), 0))
    else:
        halo = st16
        halo_spec = pl.BlockSpec((bb, POOL_HALO, POOL_WIDTH), lambda b, i: (b, 0, 0))
    return pl.pallas_call(
        functools.partial(_pool_kernel, pos0),
        grid=(B // bb, nl),
        in_specs=[pl.BlockSpec((bb, tl, POOL_WIDTH), lambda b, i: (b, i, 0)),
                  halo_spec,
                  pl.BlockSpec((bb, POOL_HALO, POOL_WIDTH), lambda b, i: (b, 0, 0)),
                  pl.BlockSpec((len(POOL_WINDOWS), POOL_GROUP, POOL_GROUP), lambda b, i: (0, 0, 0)),
                  pl.BlockSpec((1, POOL_WIDTH), lambda b, i: (0, 0))],
        out_specs=[pl.BlockSpec((bb * tl, POOL_WIDTH), lambda b, i: (b * nl + i, 0)),
                   pl.BlockSpec((bb, POOL_BUF, POOL_WIDTH), lambda b, i: (b, 0, 0))],
        out_shape=[jax.ShapeDtypeStruct((B * L, POOL_WIDTH), BF16),
                   jax.ShapeDtypeStruct((B, POOL_BUF, POOL_WIDTH), F32)],
        scratch_shapes=[pltpu.VMEM((bb, POOL_HALO + tl, POOL_WIDTH), F32)],
        compiler_params=_params("arbitrary", "arbitrary"),
        name="pool_mixer",
    )(p, halo, st16, w_pool, scale)


def _seg_mats():
    ch = np.arange(RWKV_WIDTH) // HEAD_DIM
    ln = np.arange(LANES)
    seg = (ch[:, None] == (ln[None, :] % N_HEADS)).astype(np.float32)
    exp = ((ln[:, None] == ch[None, :])).astype(np.float32)
    return jnp.asarray(seg), jnp.asarray(exp)


def _pre_kernel(z_ref, zp_ref, st_ref, mu_ref, w0_ref, w2_ref, a0_ref, a2_ref, g2_ref,
                kk_ref, ka_ref, rk_ref, seg_ref, exp_ref,
                a_o, q_o, w_o, b_o, k_o, v_o, br_o, kr_o, bs_o, g_o):
    bb, tl, _ = z_ref.shape
    i = pl.program_id(1)
    z = z_ref[...]
    first = jnp.where(i == 0, st_ref[...], zp_ref[:, SUBLANES - 1:SUBLANES, :])
    tpos = lax.broadcasted_iota(jnp.int32, (1, tl, LANES), 1)
    n = bb * tl

    def shifted(c0, c1):
        zc = z[:, :, c0:c1]
        prev = pltpu.roll(zc, 1, axis=1)
        reps = (c1 - c0) // LANES
        t0 = jnp.concatenate([tpos] * reps, axis=2) if reps > 1 else tpos
        prev = jnp.where(t0 == 0, first[:, :, c0:c1], prev)
        return (zc + (prev - zc) * mu_ref[:, c0:c1]).reshape(n, c1 - c0)

    W = RWKV_WIDTH
    r = shifted(0, W)
    k = shifted(W, 2 * W)
    v = shifted(2 * W, 3 * W)
    lora = shifted(LORA_OFF, LORA_OFF + LANES)
    gd = shifted(GATE_OFF, Z_WIDTH)

    w_lin = w0_ref[...] + jnp.dot(jnp.tanh(lora).astype(BF16), w2_ref[...], preferred_element_type=F32)
    a_lin = a0_ref[...] + jnp.dot(lora.astype(BF16), a2_ref[...], preferred_element_type=F32)
    g_o[...] = jnp.dot(_sigmoid(gd).astype(BF16), g2_ref[...], preferred_element_type=F32)
    w = -(jnp.maximum(-w_lin, 0.0) + jnp.log(1.0 + jnp.exp(-jnp.abs(w_lin)))) - 0.5
    decay = jnp.exp(-jnp.exp(w))
    a_sig = _sigmoid(a_lin)

    seg = seg_ref[...]

    def head_sum(x):
        return jnp.dot(x, seg, precision=HIGHEST, preferred_element_type=F32)

    kk = k * kk_ref[...]
    inv = 1.0 / jnp.maximum(jnp.sqrt(head_sum(kk * kk)), 1e-12)
    kk = kk * jnp.dot(inv, exp_ref[...], precision=HIGHEST, preferred_element_type=F32)
    k2 = k * (1.0 + (a_sig - 1.0) * ka_ref[...])
    b = kk * a_sig
    a_o[...] = -kk
    q_o[...] = decay * r
    w_o[...] = decay
    b_o[...] = b
    k_o[...] = k2
    v_o[...] = v
    br_o[...] = head_sum(b * r)
    kr_o[...] = head_sum(k2 * r)
    bs_o[...] = head_sum(r * k2 * rk_ref[...])


def _rwkv_pre(z, st_shift, lw):
    B, L, _ = z.shape
    bb, tl = _tiles(B, L)
    nl = L // tl
    pstep = tl // SUBLANES
    seg, exp = _seg_mats()
    row = lambda wdt: pl.BlockSpec((1, wdt), lambda b, i: (0, 0))
    full = lambda a: pl.BlockSpec(a.shape, lambda b, i: (0,) * a.ndim)
    out_w = pl.BlockSpec((bb * tl, RWKV_WIDTH), lambda b, i: (b * nl + i, 0))
    out_s = pl.BlockSpec((bb * tl, LANES), lambda b, i: (b * nl + i, 0))
    T = B * L
    wide = jax.ShapeDtypeStruct((T, RWKV_WIDTH), F32)
    small = jax.ShapeDtypeStruct((T, LANES), F32)
    return pl.pallas_call(
        _pre_kernel,
        grid=(B // bb, nl),
        in_specs=[pl.BlockSpec((bb, tl, Z_WIDTH), lambda b, i: (b, i, 0)),
                  pl.BlockSpec((bb, SUBLANES, Z_WIDTH), lambda b, i: (b, jnp.maximum(i * pstep - 1, 0), 0)),
                  pl.BlockSpec((bb, 1, Z_WIDTH), lambda b, i: (b, 0, 0)),
                  row(Z_WIDTH), row(RWKV_WIDTH), full(lw["w2"]), row(RWKV_WIDTH), full(lw["a2"]),
                  full(lw["g2"]), row(RWKV_WIDTH), row(RWKV_WIDTH), row(RWKV_WIDTH), full(seg), full(exp)],
        out_specs=[out_w] * 6 + [out_s] * 3 + [out_w],
        out_shape=[wide] * 6 + [small] * 3 + [wide],
        compiler_params=_params("arbitrary", "arbitrary"),
        name="rwkv_pre",
    )(z, z, st_shift, lw["mu"], lw["w0"], lw["w2"], lw["a0"], lw["a2"], lw["g2"],
      lw["k_k"], lw["k_a"], lw["r_k"], seg, exp)


def _scan_kernel(a_ref, q_ref, w_ref, b_ref, k_ref, v_ref, br_ref, kr_ref, s0_ref, y_ref, s_ref):
    tc = v_ref.shape[0]
    rows = SUBLANES * N_HEADS

    @pl.when(pl.program_id(1) == 0)
    def _():
        s_ref[...] = s0_ref[...]

    def expand(x2, t):
        blk = x2[t * N_HEADS:(t + 1) * N_HEADS]
        return jnp.concatenate([blk] * SUBLANES, axis=0).T

    def group(g, carry):
        r0 = pl.multiple_of(g * rows, rows)
        t0 = pl.multiple_of(g * SUBLANES, SUBLANES)
        aq = jnp.concatenate([a_ref[pl.ds(r0, rows), :], q_ref[pl.ds(r0, rows), :]], axis=1)
        wb = jnp.concatenate([w_ref[pl.ds(r0, rows), :], b_ref[pl.ds(r0, rows), :]], axis=1)
        kx = k_ref[pl.ds(r0, rows), :]
        kx = jnp.concatenate([kx, kx], axis=1)
        v8 = v_ref[pl.ds(t0, SUBLANES), :]
        br8 = br_ref[pl.ds(t0, SUBLANES), :]
        kr8 = kr_ref[pl.ds(t0, SUBLANES), :]
        for t in range(SUBLANES):
            aq_t = expand(aq, t)
            wb_t = expand(wb, t)
            k_e = expand(kx, t)[0:HEAD_DIM]
            a_e, q_e = aq_t[0:HEAD_DIM], aq_t[HEAD_DIM:]
            w_e, b_e = wb_t[0:HEAD_DIM], wb_t[HEAD_DIM:]
            br_t = br8[t:t + 1]
            kr_t = kr8[t:t + 1]
            ys = []
            for c in range(RWKV_WIDTH // LANES):
                cols = slice(c * LANES, (c + 1) * LANES)
                p = s_ref[0, :, cols]
                sa = jnp.sum(p * a_e, axis=0, keepdims=True)
                yq = jnp.sum(p * q_e, axis=0, keepdims=True)
                vc = v8[t:t + 1, cols]
                s_ref[0, :, cols] = p * w_e + sa * b_e + vc * k_e
                ys.append(yq + sa * br_t + vc * kr_t)
            y_ref[pl.ds(t0 + t, 1), :] = jnp.concatenate(ys, axis=1)
        return carry

    lax.fori_loop(0, tc // SUBLANES, group, 0)


def _scan(ops, s0, B, L):
    a, q, w, b, k, v, br, kr = ops
    tc = min(L, 128)
    nc = L // tc
    T = B * L
    hk = lambda x: x.reshape(T * N_HEADS, HEAD_DIM)
    op_spec = pl.BlockSpec((tc * N_HEADS, HEAD_DIM), lambda bi, c: (bi * nc + c, 0))
    v_spec = pl.BlockSpec((tc, RWKV_WIDTH), lambda bi, c: (bi * nc + c, 0))
    s_spec = pl.BlockSpec((tc, LANES), lambda bi, c: (bi * nc + c, 0))
    st_spec = pl.BlockSpec((1, HEAD_DIM, RWKV_WIDTH), lambda bi, c: (bi, 0, 0))
    return pl.pallas_call(
        _scan_kernel,
        grid=(B, nc),
        in_specs=[op_spec] * 5 + [v_spec, s_spec, s_spec, st_spec],
        out_specs=[v_spec, st_spec],
        out_shape=[jax.ShapeDtypeStruct((T, RWKV_WIDTH), F32),
                   jax.ShapeDtypeStruct((B, HEAD_DIM, RWKV_WIDTH), F32)],
        compiler_params=_params("arbitrary", "arbitrary"),
        name="rwkv_scan",
    )(hk(a), hk(q), hk(w), hk(b), hk(k), v, br, kr, s0)


def _post_kernel(y_ref, v_ref, g_ref, bs_ref, yp_ref, x_ref, lw_ref, lb_ref, wo_ref,
                 gta_ref, n2_ref, sh_ref, sc_ref, wr_ref, x1_ref, h2_ref, s_ref):
    bb, tl, _ = x_ref.shape
    nblk = RWKV_WIDTH // LANES
    y = y_ref[...]
    cols = [slice(c * LANES, (c + 1) * LANES) for c in range(nblk)]

    def head_total(parts):
        s = parts[0]
        for part in parts[1:]:
            s = s + part
        for sh in (N_HEADS, 2 * N_HEADS, 4 * N_HEADS):
            s = s + pltpu.roll(s, sh, axis=1)
        return s

    mean = head_total([y[:, c] for c in cols]) * (1.0 / HEAD_DIM)
    dev = [y[:, c] - mean for c in cols]
    var = head_total([d * d for d in dev]) * (1.0 / HEAD_DIM)
    rstd = lax.rsqrt(var + LNX_EPS)
    bs = bs_ref[...]
    outs = []
    for ci, c in enumerate(cols):
        yn = dev[ci] * rstd * lw_ref[:, c] + lb_ref[:, c]
        outs.append(((yn + bs * v_ref[:, c]) * g_ref[:, c]).astype(BF16))
    yr = jnp.concatenate(outs, axis=1)
    mix = jnp.dot(yp_ref[...], wo_ref[0:POOL_WIDTH, :], preferred_element_type=F32)
    mix = mix + jnp.dot(yr, wo_ref[POOL_WIDTH:, :], preferred_element_type=F32)
    x1 = x_ref[...] + gta_ref[...] * mix.reshape(bb, tl, D_MODEL)
    x1_ref[...] = x1
    h2 = _ada_norm_val(x1, n2_ref[...], sh_ref[...], sc_ref[...]).reshape(bb * tl, D_MODEL)
    h2_ref[...] = h2.astype(BF16)
    s_ref[...] = _sigmoid(jnp.dot(h2, wr_ref[...], precision=HIGHEST, preferred_element_type=F32))


def _post(y, v, g, bs, ypool, x, lw, gta, sh_m, sc_m, w_router_p):
    B, L, _ = x.shape
    bb, tl = _tiles(B, L)
    nl = L // tl
    T = B * L
    rows = lambda wdt: pl.BlockSpec((bb * tl, wdt), lambda b, i: (b * nl + i, 0))
    row = lambda wdt: pl.BlockSpec((1, wdt), lambda b, i: (0, 0))
    mod = pl.BlockSpec((bb, 1, D_MODEL), lambda b, i: (b, 0, 0))
    xs = pl.BlockSpec((bb, tl, D_MODEL), lambda b, i: (b, i, 0))
    return pl.pallas_call(
        _post_kernel,
        grid=(B // bb, nl),
        in_specs=[rows(RWKV_WIDTH), rows(RWKV_WIDTH), rows(RWKV_WIDTH), rows(LANES), rows(POOL_WIDTH), xs,
                  row(RWKV_WIDTH), row(RWKV_WIDTH),
                  pl.BlockSpec((D_MODEL, D_MODEL), lambda b, i: (0, 0)),
                  mod, row(D_MODEL), mod, mod,
                  pl.BlockSpec((D_MODEL, LANES), lambda b, i: (0, 0))],
        out_specs=[xs, rows(D_MODEL), rows(LANES)],
        out_shape=[jax.ShapeDtypeStruct((B, L, D_MODEL), F32),
                   jax.ShapeDtypeStruct((T, D_MODEL), BF16),
                   jax.ShapeDtypeStruct((T, LANES), F32)],
        compiler_params=_params("arbitrary", "arbitrary"),
        name="post_outproj_norm2",
    )(y, v, g, bs, ypool, x, lw["lnx_w"], lw["lnx_b"], lw["w_out"], gta, lw["norm2_g"], sh_m, sc_m, w_router_p)


def _expert_changed(be_ref, b):
    return jnp.logical_or(b == 0, be_ref[b] != be_ref[jnp.maximum(b - 1, 0)])


def _gateup_kernel(be_ref, nv_ref, x_ref, wg_ref, wu_ref, o_ref, wgc, wuc):
    b = pl.program_id(1)

    @pl.when(_expert_changed(be_ref, b))
    def _():
        wgc[...] = wg_ref[0].astype(BF16)
        wuc[...] = wu_ref[0].astype(BF16)

    @pl.when(b < nv_ref[0])
    def _():
        x = x_ref[...]
        gate = jnp.dot(x, wgc[...], preferred_element_type=F32)
        up = jnp.dot(x, wuc[...], preferred_element_type=F32)
        o_ref[...] = (gate * _sigmoid(gate) * up).astype(BF16)

    @pl.when(b >= nv_ref[0])
    def _():
        o_ref[...] = jnp.zeros_like(o_ref)


def _down_kernel(be_ref, nv_ref, a_ref, wd_ref, rw_ref, o_ref, wdc):
    b = pl.program_id(1)

    @pl.when(_expert_changed(be_ref, b))
    def _():
        wdc[...] = wd_ref[0].astype(BF16)

    @pl.when(b < nv_ref[0])
    def _():
        y = jnp.dot(a_ref[...], wdc[...], preferred_element_type=F32)
        rw = rw_ref[...]
        for c in range(o_ref.shape[1] // LANES):
            cols = slice(c * LANES, (c + 1) * LANES)
            o_ref[:, cols] = y[:, cols] * rw

    @pl.when(b >= nv_ref[0])
    def _():
        o_ref[...] = jnp.zeros_like(o_ref)


def _moe_matmuls(xs, block_e, n_valid, row_w128, w_gate, w_up, w_down):
    R = xs.shape[0]
    nb = R // MOE_BLOCK
    ce = 512
    act = pl.pallas_call(
        _gateup_kernel,
        grid_spec=pltpu.PrefetchScalarGridSpec(
            num_scalar_prefetch=2,
            grid=(D_EXPERT // ce, nb),
            in_specs=[pl.BlockSpec((MOE_BLOCK, D_MODEL), lambda c, b, be, nv: (b, 0)),
                      pl.BlockSpec((1, D_MODEL, ce), lambda c, b, be, nv: (be[b], 0, c)),
                      pl.BlockSpec((1, D_MODEL, ce), lambda c, b, be, nv: (be[b], 0, c))],
            out_specs=pl.BlockSpec((MOE_BLOCK, ce), lambda c, b, be, nv: (b, c)),
            scratch_shapes=[pltpu.VMEM((D_MODEL, ce), BF16), pltpu.VMEM((D_MODEL, ce), BF16)]),
        out_shape=jax.ShapeDtypeStruct((R, D_EXPERT), BF16),
        compiler_params=_params("arbitrary", "arbitrary"),
        name="moe_gate_up",
    )(block_e, n_valid, xs, w_gate, w_up)
    cd = 1024
    return pl.pallas_call(
        _down_kernel,
        grid_spec=pltpu.PrefetchScalarGridSpec(
            num_scalar_prefetch=2,
            grid=(D_MODEL // cd, nb),
            in_specs=[pl.BlockSpec((MOE_BLOCK, D_EXPERT), lambda c, b, be, nv: (b, 0)),
                      pl.BlockSpec((1, D_EXPERT, cd), lambda c, b, be, nv: (be[b], 0, c)),
                      pl.BlockSpec((MOE_BLOCK, LANES), lambda c, b, be, nv: (b, 0))],
            out_specs=pl.BlockSpec((MOE_BLOCK, cd), lambda c, b, be, nv: (b, c)),
            scratch_shapes=[pltpu.VMEM((D_EXPERT, cd), BF16)]),
        out_shape=jax.ShapeDtypeStruct((R, D_MODEL), F32),
        compiler_params=_params("arbitrary", "arbitrary"),
        name="moe_down",
    )(block_e, n_valid, act, w_down, row_w128)


def _route(scores, b_router):
    T = scores.shape[0]
    sel = scores + b_router.astype(F32)
    gscore = jnp.sum(lax.top_k(sel.reshape(T, N_EXPERT_GROUPS, EXPERTS_PER_GROUP), 2)[0], -1)
    gbest = jnp.argmax(gscore, -1)
    in_group = (jnp.arange(N_EXPERTS) // EXPERTS_PER_GROUP)[None, :] == gbest[:, None]
    _, idx = lax.top_k(jnp.where(in_group, sel, -jnp.inf), TOP_K)
    wts = jnp.take_along_axis(scores, idx, 1)
    wts = wts / jnp.sum(wts, -1, keepdims=True)
    return idx.astype(jnp.int32), wts


def _dispatch(idx, wts):
    T = idx.shape[0]
    A = T * TOP_K
    nb = -(-(A + N_EXPERTS * (MOE_BLOCK - 1)) // MOE_BLOCK)
    R = nb * MOE_BLOCK
    e_flat = idx.reshape(A)
    onehot = (e_flat[:, None] == jnp.arange(N_EXPERTS, dtype=jnp.int32)[None, :]).astype(jnp.int32)
    csum = jnp.cumsum(onehot, axis=0)
    rank = jnp.sum(csum * onehot, axis=1) - 1
    counts = csum[-1]
    padded = (counts + MOE_BLOCK - 1) // MOE_BLOCK * MOE_BLOCK
    pend = jnp.cumsum(padded)
    pstart = pend - padded
    dest = (pstart[e_flat] + rank).astype(jnp.int32)
    tok_flat = jnp.repeat(jnp.arange(T, dtype=jnp.int32), TOP_K)
    row_tok = jnp.zeros((R,), jnp.int32).at[dest].set(tok_flat)
    row_w = jnp.zeros((R,), F32).at[dest].set(wts.reshape(A))
    block_e = jnp.searchsorted(pend, jnp.arange(nb, dtype=jnp.int32) * MOE_BLOCK, side="right")
    block_e = jnp.minimum(block_e, N_EXPERTS - 1).astype(jnp.int32)
    n_valid = (pend[-1] // MOE_BLOCK).astype(jnp.int32).reshape(1)
    return row_tok, row_w, block_e, n_valid, dest.reshape(T, TOP_K)


def _combine_kernel(final, x_ref, y0_ref, y1_ref, gt_ref, fg_ref, o_ref):
    bb, tl, _ = x_ref.shape
    ff = (y0_ref[...] + y1_ref[...]).reshape(bb, tl, D_MODEL)
    x2 = x_ref[...] + gt_ref[...] * ff
    if final:
        ms = jnp.mean(x2 * x2, axis=-1, keepdims=True)
        x2 = x2 * lax.rsqrt(ms + RMS_EPS) * fg_ref[...]
    o_ref[...] = x2


def _combine(x1, y0, y1, gt_m, final_g, final):
    B, L, _ = x1.shape
    bb, tl = _tiles(B, L)
    nl = L // tl
    xs = pl.BlockSpec((bb, tl, D_MODEL), lambda b, i: (b, i, 0))
    rows = pl.BlockSpec((bb * tl, D_MODEL), lambda b, i: (b * nl + i, 0))
    return pl.pallas_call(
        functools.partial(_combine_kernel, final),
        grid=(B // bb, nl),
        in_specs=[xs, rows, rows, pl.BlockSpec((bb, 1, D_MODEL), lambda b, i: (b, 0, 0)),
                  pl.BlockSpec((1, D_MODEL), lambda b, i: (0, 0))],
        out_specs=xs,
        out_shape=jax.ShapeDtypeStruct((B, L, D_MODEL), F32),
        compiler_params=_params("arbitrary", "arbitrary"),
        name="moe_combine",
    )(x1, y0, y1, gt_m, final_g)


def _vh(x, axis=-1):
    return jnp.take(x, jnp.asarray(VH_PERM), axis=axis)


def _layer_weights(l, w_in, w_pool, pool_scale, mu_shift, w0, w2, a0, a2, g2, k_k, k_a, r_k,
                   lnx_w, lnx_b, w_out, norm1_g, norm2_g):
    W = RWKV_WIDTH
    wz = w_in[l][:, POOL_WIDTH:]
    wz = jnp.concatenate([wz[:, :2 * W], _vh(wz[:, 2 * W:3 * W]), wz[:, 3 * W:],
                          jnp.zeros((D_MODEL, Z_WIDTH - SHIFT_WIDTH), F32)], axis=1).astype(BF16)
    mu = mu_shift[l]
    mu = jnp.concatenate([mu[:2 * W], _vh(mu[2 * W:3 * W]), mu[3 * W:],
                          jnp.zeros((Z_WIDTH - SHIFT_WIDTH,), F32)]).reshape(1, Z_WIDTH)
    zpad = lambda m, top, tot: jnp.concatenate(
        [jnp.zeros((top, W), F32), m, jnp.zeros((tot - top - m.shape[0], W), F32)], axis=0).astype(BF16)
    wo = w_out[l]
    wo = jnp.concatenate([wo[:POOL_WIDTH], _vh(wo[POOL_WIDTH:], axis=0)], axis=0).astype(BF16)
    r1 = lambda x: x.reshape(1, -1)
    return dict(
        wp=w_in[l][:, :POOL_WIDTH].astype(BF16), wz=wz, mu=mu,
        w_pool=w_pool[l].astype(BF16), pool_scale=r1(pool_scale[l]),
        w0=r1(w0[l]), w2=zpad(w2[l], 0, LANES), a0=r1(a0[l]), a2=zpad(a2[l], DECAY_LORA, LANES),
        g2=zpad(_vh(g2[l]), 0, Z_WIDTH - GATE_OFF),
        k_k=r1(k_k[l]), k_a=r1(k_a[l]), r_k=r1(r_k[l]),
        lnx_w=r1(_vh(lnx_w[l])), lnx_b=r1(_vh(lnx_b[l])), w_out=wo,
        norm1_g=r1(norm1_g[l]), norm2_g=r1(norm2_g[l]))


def _shift_to_z(s):
    W = RWKV_WIDTH
    out = jnp.concatenate([s[:, :2 * W], _vh(s[:, 2 * W:3 * W]), s[:, 3 * W:],
                           jnp.zeros((s.shape[0], Z_WIDTH - SHIFT_WIDTH), s.dtype)], axis=1)
    return out[:, None, :]


def _z_to_shift(zrow):
    W = RWKV_WIDTH
    v = jnp.take(zrow[:, 2 * W:3 * W], jnp.asarray(VH_INV), axis=1)
    return jnp.concatenate([zrow[:, :2 * W], v, zrow[:, 3 * W:SHIFT_WIDTH]], axis=1)


def _state_to_tile(s):
    B = s.shape[0]
    return s.transpose(0, 3, 2, 1).reshape(B, HEAD_DIM, RWKV_WIDTH)


def _tile_to_state(p):
    B = p.shape[0]
    return p.reshape(B, HEAD_DIM, HEAD_DIM, N_HEADS).transpose(0, 3, 2, 1)


def _mix_half(x, mods, st_wkv, st_pool, st_shift, pos0, lw, w_router_p):
    B, L, _ = x.shape
    sh_a, sc_a, gt_a, sh_m, sc_m, _ = mods
    p, z = _inproj(x, lw["norm1_g"], sh_a, sc_a, lw["wp"], lw["wz"])
    st16 = jnp.concatenate([jnp.zeros((B, 1, POOL_WIDTH), F32), st_pool], axis=1)
    ypool, buf = _pool(p, st16, lw["w_pool"], lw["pool_scale"], pos0)
    a, q, w, b, k, v, br, kr, bs, g = _rwkv_pre(z, _shift_to_z(st_shift), lw)
    y, s_fin = _scan((a, q, w, b, k, v, br, kr), _state_to_tile(st_wkv), B, L)
    x1, h2, scores = _post(y, v, g, bs, ypool, x, lw, gt_a, sh_m, sc_m, w_router_p)
    return x1, h2, scores, _tile_to_state(s_fin), buf, _z_to_shift(z[:, -1])


def kernel(x_prompt, x_sample, state_wkv, state_pool, state_shift, c_prompt, c_sample, w_ada, b_ada, norm1_g, norm2_g, w_in, w_pool, pool_scale, mu_shift, w0, w2, a0, a2, g2, k_k, k_a, r_k, lnx_w, lnx_b, w_out, w_router, b_router, w_gate, w_up, w_down, final_g):
    Bp, Lp, _ = x_prompt.shape
    Bs, Ls, _ = x_sample.shape
    Tp, Ts = Bp * Lp, Bs * Ls
    mod = _ada(jnp.concatenate([c_prompt, c_sample], axis=0), w_ada, b_ada)
    w_router_p = jnp.concatenate([w_router, jnp.zeros((D_MODEL, LANES - N_EXPERTS), F32)], axis=1)
    fg = final_g.reshape(1, D_MODEL)

    xp, xs_ = x_prompt, x_sample
    outs_p, outs_s = [], []
    for l in range(DEPTH):
        lw = _layer_weights(l, w_in, w_pool, pool_scale, mu_shift, w0, w2, a0, a2, g2, k_k, k_a, r_k,
                            lnx_w, lnx_b, w_out, norm1_g, norm2_g)
        mods = [mod[l][:, None, i * D_MODEL:(i + 1) * D_MODEL] for i in range(6)]
        mods_p = [m[:Bp] for m in mods]
        mods_s = [m[Bp:] for m in mods]
        zp = lambda *s: jnp.zeros(s, F32)
        x1p, h2p, scp, wkv_p, buf_p, shf_p = _mix_half(
            xp, mods_p, zp(Bp, N_HEADS, HEAD_DIM, HEAD_DIM), zp(Bp, POOL_BUF, POOL_WIDTH),
            zp(Bp, SHIFT_WIDTH), 0, lw, w_router_p)
        x1s, h2s, scs, wkv_s, buf_s, shf_s = _mix_half(
            xs_, mods_s, state_wkv[l], state_pool[l], state_shift[l], PAST_LEN, lw, w_router_p)
        outs_p.append((wkv_p, buf_p, shf_p))
        outs_s.append((wkv_s, buf_s, shf_s))

        h2 = jnp.concatenate([h2p, h2s], axis=0)
        scores = jnp.concatenate([scp, scs], axis=0)[:, :N_EXPERTS]
        idx, wts = _route(scores, b_router)
        row_tok, row_w, block_e, n_valid, pos = _dispatch(idx, wts)
        xs_rows = jnp.take(h2, row_tok, axis=0)
        row_w128 = jnp.broadcast_to(row_w[:, None], (row_w.shape[0], LANES))
        yb = _moe_matmuls(xs_rows, block_e, n_valid, row_w128, w_gate[l], w_up[l], w_down[l])
        y0 = jnp.take(yb, pos[:, 0], axis=0)
        y1 = jnp.take(yb, pos[:, 1], axis=0)
        final = l == DEPTH - 1
        xp = _combine(x1p, y0[:Tp], y1[:Tp], mods_p[5], fg, final)
        xs_ = _combine(x1s, y0[Tp:], y1[Tp:], mods_s[5], fg, final)

    st = lambda outs, i: jnp.stack([o[i] for o in outs])
    return (xp, xs_, st(outs_p, 0), st(outs_p, 1), st(outs_p, 2),
            st(outs_s, 0), st(outs_s, 1), st(outs_s, 2))
```

```python
import functools

import numpy as np
import jax
import jax.numpy as jnp
from jax import lax
from jax.experimental import pallas as pl
from jax.experimental.pallas import tpu as pltpu

F32 = jnp.float32
BF16 = jnp.bfloat16

D_MODEL = 2048
DEPTH = 4
PAST_LEN = 16384
POOL_WIDTH = 1024
POOL_WINDOWS = (2, 4, 8, 16)
POOL_GROUP = 256
POOL_BUF = 15
POOL_HALO = 16
RWKV_WIDTH = 1024
HEAD_DIM = 64
N_HEADS = 16
DECAY_LORA = 64
AAA_LORA = 64
GATE_LORA = 160
SHIFT_WIDTH = 3 * RWKV_WIDTH + DECAY_LORA + AAA_LORA + GATE_LORA
Z_WIDTH = 3456
LORA_OFF = 3 * RWKV_WIDTH
GATE_OFF = LORA_OFF + 128
LNX_EPS = 64e-5
RMS_EPS = 1e-6
N_EXPERTS = 16
N_EXPERT_GROUPS = 4
EXPERTS_PER_GROUP = 4
TOP_K = 2
D_EXPERT = 1024

LANES = 128
SUBLANES = 8
ROW_TILE = 256
MOE_BLOCK = 256
VMEM_LIMIT = 56 * 1024 * 1024
HIGHEST = lax.Precision.HIGHEST

_P = np.arange(RWKV_WIDTH)
VH_PERM = (_P % N_HEADS) * HEAD_DIM + _P // N_HEADS
_C = np.arange(RWKV_WIDTH)
VH_INV = (_C % HEAD_DIM) * N_HEADS + _C // HEAD_DIM


def _params(*sem):
    return pltpu.CompilerParams(dimension_semantics=sem, vmem_limit_bytes=VMEM_LIMIT)


def _sigmoid(x):
    return 1.0 / (1.0 + jnp.exp(-x))


def _tiles(B, L):
    if L >= ROW_TILE:
        assert L % ROW_TILE == 0
        return 1, ROW_TILE
    assert ROW_TILE % L == 0 and L % SUBLANES == 0 and B % (ROW_TILE // L) == 0
    return ROW_TILE // L, L


def _mod_spec(bb, l, k):
    return pl.BlockSpec((None, None, bb, 1, D_MODEL), lambda b, i, *_: (l, k, b, 0, 0))


def _ada_kernel(c_ref, w_ref, b_ref, o_ref):
    c = c_ref[...]
    s = (c * _sigmoid(c)).astype(BF16)
    o_ref[0] = jnp.dot(s, w_ref[0].astype(BF16), preferred_element_type=F32) + b_ref[0]


def _ada(c_all, w_ada, b_ada):
    bc = c_all.shape[0]
    tn = 512
    return pl.pallas_call(
        _ada_kernel,
        grid=(DEPTH, 6 * D_MODEL // tn),
        in_specs=[pl.BlockSpec((bc, D_MODEL), lambda l, n: (0, 0)),
                  pl.BlockSpec((1, D_MODEL, tn), lambda l, n: (l, 0, n)),
                  pl.BlockSpec((1, 1, tn), lambda l, n: (l, 0, n))],
        out_specs=pl.BlockSpec((1, bc, tn), lambda l, n: (l, 0, n)),
        out_shape=jax.ShapeDtypeStruct((DEPTH, bc, 6 * D_MODEL), F32),
        compiler_params=_params("arbitrary", "arbitrary"),
        name="ada_mod",
    )(c_all, w_ada, b_ada.reshape(DEPTH, 1, 6 * D_MODEL))


def _ada_norm_val(x, g, shift, scale):
    ms = jnp.mean(x * x, axis=-1, keepdims=True)
    return x * lax.rsqrt(ms + RMS_EPS) * g * (1.0 + scale) + shift


def _inproj_kernel(x_ref, g_ref, sh_ref, sc_ref, wp_ref, wz_ref, p_ref, z_ref):
    bb, tl, _ = x_ref.shape
    h = _ada_norm_val(x_ref[...], g_ref[...], sh_ref[...], sc_ref[...])
    h = h.reshape(bb * tl, D_MODEL).astype(BF16)
    p_ref[...] = jnp.dot(h, wp_ref[...], preferred_element_type=F32).reshape(bb, tl, POOL_WIDTH)
    step = 1152
    for n0 in range(0, Z_WIDTH, step):
        z_ref[:, :, n0:n0 + step] = jnp.dot(
            h, wz_ref[:, n0:n0 + step], preferred_element_type=F32).reshape(bb, tl, step)


def _inproj(x, g, mod, l, wp, wz):
    B, L, _ = x.shape
    bb, tl = _tiles(B, L)
    once = pl.Buffered(1)
    return pl.pallas_call(
        _inproj_kernel,
        grid=(B // bb, L // tl),
        in_specs=[pl.BlockSpec((bb, tl, D_MODEL), lambda b, i: (b, i, 0)),
                  pl.BlockSpec((1, D_MODEL), lambda b, i: (0, 0)),
                  _mod_spec(bb, l, 0), _mod_spec(bb, l, 1),
                  pl.BlockSpec((D_MODEL, POOL_WIDTH), lambda b, i: (0, 0), pipeline_mode=once),
                  pl.BlockSpec((D_MODEL, Z_WIDTH), lambda b, i: (0, 0), pipeline_mode=once)],
        out_specs=[pl.BlockSpec((bb, tl, POOL_WIDTH), lambda b, i: (b, i, 0)),
                   pl.BlockSpec((bb, tl, Z_WIDTH), lambda b, i: (b, i, 0))],
        out_shape=[jax.ShapeDtypeStruct((B, L, POOL_WIDTH), F32),
                   jax.ShapeDtypeStruct((B, L, Z_WIDTH), F32)],
        compiler_params=_params("arbitrary", "arbitrary"),
        name="norm_inproj",
    )(x, g, mod, mod, wp, wz)


def _pool_kernel(pos0, p_ref, halo_ref, st_ref, w_ref, scale_ref, y_ref, buf_ref, ext_ref):
    bb, tl, _ = p_ref.shape
    i = pl.program_id(1)

    @pl.when(i == 0)
    def _():
        ext_ref[:, 0:POOL_HALO, :] = st_ref[...]

    @pl.when(i > 0)
    def _():
        ext_ref[:, 0:POOL_HALO, :] = halo_ref[...]

    ext_ref[:, POOL_HALO:POOL_HALO + tl, :] = p_ref[...]
    pos = pos0 + i * tl + lax.broadcasted_iota(jnp.int32, (1, tl, POOL_GROUP), 1)
    for gi, win in enumerate(POOL_WINDOWS):
        c0 = gi * POOL_GROUP
        cur = p_ref[:, :, c0:c0 + POOL_GROUP]
        acc = cur
        for s in range(1, win):
            acc = acc + ext_ref[:, POOL_HALO - s:POOL_HALO - s + tl, c0:c0 + POOL_GROUP]
        cnt = jnp.minimum(pos + 1, win).astype(F32)
        pooled = (acc / cnt - cur).reshape(bb * tl, POOL_GROUP).astype(BF16)
        out = jnp.dot(pooled, w_ref[gi], preferred_element_type=F32) * scale_ref[:, c0:c0 + POOL_GROUP]
        y_ref[:, c0:c0 + POOL_GROUP] = out.astype(BF16)

    @pl.when(i == pl.num_programs(1) - 1)
    def _():
        buf_ref[...] = ext_ref[:, tl + 1:tl + POOL_HALO, :]


def _pool(p, st16, w_pool, scale, pos0):
    B, L, _ = p.shape
    bb, tl = _tiles(B, L)
    nl = L // tl
    if L >= POOL_HALO:
        halo, hstep = p, tl // POOL_HALO
        halo_spec = pl.BlockSpec((bb, POOL_HALO, POOL_WIDTH),
                                 lambda b, i: (b, jnp.maximum(i * hstep - 1, 0), 0))
    else:
        halo = st16
        halo_spec = pl.BlockSpec((bb, POOL_HALO, POOL_WIDTH), lambda b, i: (b, 0, 0))
    return pl.pallas_call(
        functools.partial(_pool_kernel, pos0),
        grid=(B // bb, nl),
        in_specs=[pl.BlockSpec((bb, tl, POOL_WIDTH), lambda b, i: (b, i, 0)),
                  halo_spec,
                  pl.BlockSpec((bb, POOL_HALO, POOL_WIDTH), lambda b, i: (b, 0, 0)),
                  pl.BlockSpec((len(POOL_WINDOWS), POOL_GROUP, POOL_GROUP), lambda b, i: (0, 0, 0)),
                  pl.BlockSpec((1, POOL_WIDTH), lambda b, i: (0, 0))],
        out_specs=[pl.BlockSpec((bb * tl, POOL_WIDTH), lambda b, i: (b * nl + i, 0)),
                   pl.BlockSpec((bb, POOL_BUF, POOL_WIDTH), lambda b, i: (b, 0, 0))],
        out_shape=[jax.ShapeDtypeStruct((B * L, POOL_WIDTH), BF16),
                   jax.ShapeDtypeStruct((B, POOL_BUF, POOL_WIDTH), F32)],
        scratch_shapes=[pltpu.VMEM((bb, POOL_HALO + tl, POOL_WIDTH), F32)],
        compiler_params=_params("arbitrary", "arbitrary"),
        name="pool_mixer",
    )(p, halo, st16, w_pool, scale)


def _seg_mats():
    ch = np.arange(RWKV_WIDTH) // HEAD_DIM
    ln = np.arange(LANES)
    seg = (ch[:, None] == (ln[None, :] % N_HEADS)).astype(np.float32)
    exp = ((ln[:, None] == ch[None, :])).astype(np.float32)
    return jnp.asarray(seg), jnp.asarray(exp)


def _pre_kernel(z_ref, zp_ref, st_ref, mu_ref, w0_ref, w2_ref, a0_ref, a2_ref, g2_ref,
                kk_ref, ka_ref, rk_ref, seg_ref, exp_ref,
                aq_o, wb_o, kk_o, v_o, br_o, kr_o, bs_o, g_o):
    bb, tl, _ = z_ref.shape
    i = pl.program_id(1)
    z = z_ref[...]
    first = jnp.where(i == 0, st_ref[...], zp_ref[:, SUBLANES - 1:SUBLANES, :])
    tpos = lax.broadcasted_iota(jnp.int32, (1, tl, LANES), 1)
    n = bb * tl

    def shifted(c0, c1):
        zc = z[:, :, c0:c1]
        prev = pltpu.roll(zc, 1, axis=1)
        reps = (c1 - c0) // LANES
        t0 = jnp.concatenate([tpos] * reps, axis=2) if reps > 1 else tpos
        prev = jnp.where(t0 == 0, first[:, :, c0:c1], prev)
        return (zc + (prev - zc) * mu_ref[:, c0:c1]).reshape(n, c1 - c0)

    W = RWKV_WIDTH
    r = shifted(0, W)
    k = shifted(W, 2 * W)
    v = shifted(2 * W, 3 * W)
    lora = shifted(LORA_OFF, LORA_OFF + LANES)
    gd = shifted(GATE_OFF, Z_WIDTH)

    w_lin = w0_ref[...] + jnp.dot(jnp.tanh(lora).astype(BF16), w2_ref[...], preferred_element_type=F32)
    a_lin = a0_ref[...] + jnp.dot(lora.astype(BF16), a2_ref[...], preferred_element_type=F32)
    g_o[...] = jnp.dot(_sigmoid(gd).astype(BF16), g2_ref[...], preferred_element_type=F32)
    w = -(jnp.maximum(-w_lin, 0.0) + jnp.log(1.0 + jnp.exp(-jnp.abs(w_lin)))) - 0.5
    decay = jnp.exp(-jnp.exp(w))
    a_sig = _sigmoid(a_lin)

    seg = seg_ref[...]

    def head_sum(x):
        return jnp.dot(x, seg, precision=HIGHEST, preferred_element_type=F32)

    kk = k * kk_ref[...]
    inv = 1.0 / jnp.maximum(jnp.sqrt(head_sum(kk * kk)), 1e-12)
    kk = kk * jnp.dot(inv, exp_ref[...], precision=HIGHEST, preferred_element_type=F32)
    k2 = k * (1.0 + (a_sig - 1.0) * ka_ref[...])
    b = kk * a_sig

    low_half = lax.broadcasted_iota(jnp.int32, (n, LANES), 1) < HEAD_DIM

    def store_pair(o_ref, xa, xb):
        for c in range(RWKV_WIDTH // LANES):
            cols = slice(c * LANES, (c + 1) * LANES)
            ac, bc = xa[:, cols], xb[:, cols]
            a_sw = pltpu.roll(ac, HEAD_DIM, axis=1)
            b_sw = pltpu.roll(bc, HEAD_DIM, axis=1)
            o_ref[pl.ds(2 * c, n, stride=N_HEADS), :] = jnp.where(low_half, ac, b_sw)
            o_ref[pl.ds(2 * c + 1, n, stride=N_HEADS), :] = jnp.where(low_half, a_sw, bc)

    store_pair(aq_o, -kk, decay * r)
    store_pair(wb_o, decay, b)
    store_pair(kk_o, k2, k2)
    v_o[...] = v
    br_o[...] = head_sum(b * r)
    kr_o[...] = head_sum(k2 * r)
    bs_o[...] = head_sum(r * k2 * rk_ref[...])


def _rwkv_pre(z, st_shift, lw):
    B, L, _ = z.shape
    bb, tl = _tiles(B, L)
    nl = L // tl
    pstep = tl // SUBLANES
    seg, exp = _seg_mats()
    row = lambda wdt: pl.BlockSpec((1, wdt), lambda b, i: (0, 0))
    full = lambda a: pl.BlockSpec(a.shape, lambda b, i: (0,) * a.ndim)
    out_w = pl.BlockSpec((bb * tl, RWKV_WIDTH), lambda b, i: (b * nl + i, 0))
    out_s = pl.BlockSpec((bb * tl, LANES), lambda b, i: (b * nl + i, 0))
    out_p = pl.BlockSpec((bb * tl * N_HEADS, LANES), lambda b, i: (b * nl + i, 0))
    T = B * L
    wide = jax.ShapeDtypeStruct((T, RWKV_WIDTH), F32)
    pair = jax.ShapeDtypeStruct((T * N_HEADS, LANES), F32)
    small = jax.ShapeDtypeStruct((T, LANES), F32)
    return pl.pallas_call(
        _pre_kernel,
        grid=(B // bb, nl),
        in_specs=[pl.BlockSpec((bb, tl, Z_WIDTH), lambda b, i: (b, i, 0)),
                  pl.BlockSpec((bb, SUBLANES, Z_WIDTH), lambda b, i: (b, jnp.maximum(i * pstep - 1, 0), 0)),
                  pl.BlockSpec((bb, 1, Z_WIDTH), lambda b, i: (b, 0, 0)),
                  row(Z_WIDTH), row(RWKV_WIDTH), full(lw["w2"]), row(RWKV_WIDTH), full(lw["a2"]),
                  full(lw["g2"]), row(RWKV_WIDTH), row(RWKV_WIDTH), row(RWKV_WIDTH), full(seg), full(exp)],
        out_specs=[out_p] * 3 + [out_w] + [out_s] * 3 + [out_w],
        out_shape=[pair] * 3 + [wide] + [small] * 3 + [wide],
        compiler_params=_params("arbitrary", "arbitrary"),
        name="rwkv_pre",
    )(z, z, st_shift, lw["mu"], lw["w0"], lw["w2"], lw["a0"], lw["a2"], lw["g2"],
      lw["k_k"], lw["k_a"], lw["r_k"], seg, exp)


def _scan_kernel(aq_ref, wb_ref, kk_ref, v_ref, br_ref, kr_ref, s0_ref, y_ref, s_ref):
    tc = v_ref.shape[0]
    rows = SUBLANES * N_HEADS

    @pl.when(pl.program_id(1) == 0)
    def _():
        s_ref[...] = s0_ref[...]

    def expand(x2, t):
        blk = x2[t * N_HEADS:(t + 1) * N_HEADS]
        return jnp.concatenate([blk] * SUBLANES, axis=0).T

    def group(g, carry):
        r0 = pl.multiple_of(g * rows, rows)
        t0 = pl.multiple_of(g * SUBLANES, SUBLANES)
        aq = aq_ref[pl.ds(r0, rows), :]
        wb = wb_ref[pl.ds(r0, rows), :]
        kx = kk_ref[pl.ds(r0, rows), :]
        v8 = v_ref[pl.ds(t0, SUBLANES), :]
        br8 = br_ref[pl.ds(t0, SUBLANES), :]
        kr8 = kr_ref[pl.ds(t0, SUBLANES), :]
        for t in range(SUBLANES):
            aq_t = expand(aq, t)
            wb_t = expand(wb, t)
            k_e = expand(kx, t)[0:HEAD_DIM]
            a_e, q_e = aq_t[0:HEAD_DIM], aq_t[HEAD_DIM:]
            w_e, b_e = wb_t[0:HEAD_DIM], wb_t[HEAD_DIM:]
            br_t = br8[t:t + 1]
            kr_t = kr8[t:t + 1]
            ys = []
            for c in range(RWKV_WIDTH // LANES):
                cols = slice(c * LANES, (c + 1) * LANES)
                p = s_ref[0, :, cols]
                sa = jnp.sum(p * a_e, axis=0, keepdims=True)
                yq = jnp.sum(p * q_e, axis=0, keepdims=True)
                vc = v8[t:t + 1, cols]
                s_ref[0, :, cols] = p * w_e + sa * b_e + vc * k_e
                ys.append(yq + sa * br_t + vc * kr_t)
            y_ref[pl.ds(t0 + t, 1), :] = jnp.concatenate(ys, axis=1)
        return carry

    lax.fori_loop(0, tc // SUBLANES, group, 0)


def _scan(ops, s0, B, L):
    aq, wb, kk, v, br, kr = ops
    tc = min(L, 128)
    nc = L // tc
    T = B * L
    op_spec = pl.BlockSpec((tc * N_HEADS, LANES), lambda bi, c: (bi * nc + c, 0))
    v_spec = pl.BlockSpec((tc, RWKV_WIDTH), lambda bi, c: (bi * nc + c, 0))
    s_spec = pl.BlockSpec((tc, LANES), lambda bi, c: (bi * nc + c, 0))
    st_spec = pl.BlockSpec((1, HEAD_DIM, RWKV_WIDTH), lambda bi, c: (bi, 0, 0))
    return pl.pallas_call(
        _scan_kernel,
        grid=(B, nc),
        in_specs=[op_spec] * 3 + [v_spec, s_spec, s_spec, st_spec],
        out_specs=[v_spec, st_spec],
        out_shape=[jax.ShapeDtypeStruct((T, RWKV_WIDTH), F32),
                   jax.ShapeDtypeStruct((B, HEAD_DIM, RWKV_WIDTH), F32)],
        compiler_params=_params("arbitrary", "arbitrary"),
        name="rwkv_scan",
    )(aq, wb, kk, v, br, kr, s0)


def _post_kernel(y_ref, v_ref, g_ref, bs_ref, yp_ref, x_ref, lw_ref, lb_ref, wo_ref,
                 gta_ref, n2_ref, sh_ref, sc_ref, wr_ref, x1_ref, h2_ref, s_ref):
    bb, tl, _ = x_ref.shape
    nblk = RWKV_WIDTH // LANES
    y = y_ref[...]
    cols = [slice(c * LANES, (c + 1) * LANES) for c in range(nblk)]

    def head_total(parts):
        s = parts[0]
        for part in parts[1:]:
            s = s + part
        for sh in (N_HEADS, 2 * N_HEADS, 4 * N_HEADS):
            s = s + pltpu.roll(s, sh, axis=1)
        return s

    mean = head_total([y[:, c] for c in cols]) * (1.0 / HEAD_DIM)
    dev = [y[:, c] - mean for c in cols]
    var = head_total([d * d for d in dev]) * (1.0 / HEAD_DIM)
    rstd = lax.rsqrt(var + LNX_EPS)
    bs = bs_ref[...]
    outs = []
    for ci, c in enumerate(cols):
        yn = dev[ci] * rstd * lw_ref[:, c] + lb_ref[:, c]
        outs.append(((yn + bs * v_ref[:, c]) * g_ref[:, c]).astype(BF16))
    yr = jnp.concatenate(outs, axis=1)
    mix = jnp.dot(yp_ref[...], wo_ref[0:POOL_WIDTH, :], preferred_element_type=F32)
    mix = mix + jnp.dot(yr, wo_ref[POOL_WIDTH:, :], preferred_element_type=F32)
    x1 = x_ref[...] + gta_ref[...] * mix.reshape(bb, tl, D_MODEL)
    x1_ref[...] = x1
    h2 = _ada_norm_val(x1, n2_ref[...], sh_ref[...], sc_ref[...]).reshape(bb * tl, D_MODEL)
    h2_ref[...] = h2
    s_ref[...] = _sigmoid(jnp.dot(h2, wr_ref[...], precision=HIGHEST, preferred_element_type=F32))


def _post(y, v, g, bs, ypool, x, lw, mod, l, w_router_p):
    B, L, _ = x.shape
    bb, tl = _tiles(B, L)
    nl = L // tl
    T = B * L
    rows = lambda wdt: pl.BlockSpec((bb * tl, wdt), lambda b, i: (b * nl + i, 0))
    row = lambda wdt: pl.BlockSpec((1, wdt), lambda b, i: (0, 0))
    xs = pl.BlockSpec((bb, tl, D_MODEL), lambda b, i: (b, i, 0))
    return pl.pallas_call(
        _post_kernel,
        grid=(B // bb, nl),
        in_specs=[rows(RWKV_WIDTH), rows(RWKV_WIDTH), rows(RWKV_WIDTH), rows(LANES), rows(POOL_WIDTH), xs,
                  row(RWKV_WIDTH), row(RWKV_WIDTH),
                  pl.BlockSpec((D_MODEL, D_MODEL), lambda b, i: (0, 0)),
                  _mod_spec(bb, l, 2), row(D_MODEL), _mod_spec(bb, l, 3), _mod_spec(bb, l, 4),
                  pl.BlockSpec((D_MODEL, LANES), lambda b, i: (0, 0))],
        out_specs=[xs, rows(D_MODEL), rows(LANES)],
        out_shape=[jax.ShapeDtypeStruct((B, L, D_MODEL), F32),
                   jax.ShapeDtypeStruct((T, D_MODEL), F32),
                   jax.ShapeDtypeStruct((T, LANES), F32)],
        compiler_params=_params("arbitrary", "arbitrary"),
        name="post_outproj_norm2",
    )(y, v, g, bs, ypool, x, lw["lnx_w"], lw["lnx_b"], lw["w_out"], mod, lw["norm2_g"], mod, mod, w_router_p)


def _route(scores, b_router):
    T = scores.shape[0]
    sel = scores + b_router.astype(F32)
    neg = jnp.float32(-jnp.inf)

    def top2(x):
        lane = lax.broadcasted_iota(jnp.int32, x.shape, x.ndim - 1)
        i0 = jnp.argmax(x, -1)
        m0 = jnp.max(x, -1)
        rest = jnp.where(lane == i0[..., None], neg, x)
        return m0, jnp.max(rest, -1), i0, jnp.argmax(rest, -1)

    g0, g1, _, _ = top2(sel.reshape(T, N_EXPERT_GROUPS, EXPERTS_PER_GROUP))
    gbest = jnp.argmax(g0 + g1, -1)
    in_group = (jnp.arange(N_EXPERTS) // EXPERTS_PER_GROUP)[None, :] == gbest[:, None]
    _, _, i0, i1 = top2(jnp.where(in_group, sel, neg))
    idx = jnp.stack([i0, i1], axis=1).astype(jnp.int32)
    wts = jnp.take_along_axis(scores, idx, 1)
    wts = wts / jnp.sum(wts, -1, keepdims=True)
    return idx, wts


def _moe_rows(T):
    return -(-(T * TOP_K + N_EXPERTS * (MOE_BLOCK - 1)) // MOE_BLOCK)


def _dispatch_plan(idx):
    T = idx.shape[0]
    A = T * TOP_K
    nb = _moe_rows(T)
    e_flat = idx.reshape(A)
    experts = jnp.arange(N_EXPERTS, dtype=jnp.int32)
    onehot = (e_flat[:, None] == experts[None, :]).astype(jnp.int32)
    csum = jnp.cumsum(onehot, axis=0)
    rank = jnp.sum(csum * onehot, axis=1) - 1
    counts = csum[-1]
    padded = (counts + MOE_BLOCK - 1) // MOE_BLOCK * MOE_BLOCK
    pend = jnp.cumsum(padded)
    pstart = pend - padded
    dest = (jnp.sum(onehot * pstart[None, :], axis=1) + rank).astype(jnp.int32)
    blk0 = jnp.arange(nb, dtype=jnp.int32) * MOE_BLOCK
    block_e = jnp.sum((pend[None, :] <= blk0[:, None]).astype(jnp.int32), axis=1)
    block_e = jnp.minimum(block_e, N_EXPERTS - 1).astype(jnp.int32)
    n_valid = (pend[-1] // MOE_BLOCK).astype(jnp.int32).reshape(1)
    pad_start = jnp.concatenate([pstart + counts, pend[-1:]]).astype(jnp.int32)
    pad_cnt = jnp.concatenate([padded - counts, nb * MOE_BLOCK - pend[-1:]]).astype(jnp.int32)
    return dest, block_e, n_valid, pad_start, pad_cnt


def _dispatch_kernel(tiles_a, dest_ref, pad_start_ref, pad_cnt_ref, ha_ref, hb_ref, xs_ref, sem):
    i = pl.program_id(0)
    base = i * (TOP_K * ROW_TILE)

    def row_copy(h_ref, src_row, dst_row):
        return pltpu.make_async_copy(h_ref.at[pl.ds(src_row, 1), :], xs_ref.at[pl.ds(dst_row, 1), :], sem)

    def scatter_tile(h_ref):
        def issue(r, carry):
            for k in range(TOP_K):
                row_copy(h_ref, r, dest_ref[base + TOP_K * r + k]).start()
            return carry

        def drain(r, carry):
            for k in range(TOP_K):
                row_copy(h_ref, 0, 0).wait()
            return carry

        lax.fori_loop(0, ROW_TILE, issue, 0)
        lax.fori_loop(0, ROW_TILE, drain, 0)

    @pl.when(i < tiles_a)
    def _():
        scatter_tile(ha_ref)

    @pl.when(i >= tiles_a)
    def _():
        scatter_tile(hb_ref)

    @pl.when(i == 0)
    def _():
        for e in range(N_EXPERTS + 1):
            def pad(j, carry, e=e):
                row_copy(ha_ref, 0, pad_start_ref[e] + j).start()
                return carry

            def pad_wait(j, carry):
                row_copy(ha_ref, 0, 0).wait()
                return carry

            lax.fori_loop(0, pad_cnt_ref[e], pad, 0)
            lax.fori_loop(0, pad_cnt_ref[e], pad_wait, 0)


def _dispatch_rows(ha, hb, dest, pad_start, pad_cnt, n_rows):
    tiles_a, tiles_b = ha.shape[0] // ROW_TILE, hb.shape[0] // ROW_TILE
    return pl.pallas_call(
        functools.partial(_dispatch_kernel, tiles_a),
        grid_spec=pltpu.PrefetchScalarGridSpec(
            num_scalar_prefetch=3,
            grid=(tiles_a + tiles_b,),
            in_specs=[pl.BlockSpec((ROW_TILE, D_MODEL), lambda i, *_: (jnp.minimum(i, tiles_a - 1), 0)),
                      pl.BlockSpec((ROW_TILE, D_MODEL), lambda i, *_: (jnp.maximum(i - tiles_a, 0), 0))],
            out_specs=pl.BlockSpec(memory_space=pl.ANY),
            scratch_shapes=[pltpu.SemaphoreType.DMA(())]),
        out_shape=jax.ShapeDtypeStruct((n_rows, D_MODEL), F32),
        compiler_params=_params("arbitrary"),
        name="moe_dispatch",
    )(dest, pad_start, pad_cnt, ha, hb)


def _expert_changed(be_ref, b):
    return jnp.logical_or(b == 0, be_ref[b] != be_ref[jnp.maximum(b - 1, 0)])


def _gateup_kernel(be_ref, nv_ref, x_ref, wg_ref, wu_ref, o_ref, wgc, wuc):
    b = pl.program_id(1)

    @pl.when(_expert_changed(be_ref, b))
    def _():
        wgc[...] = wg_ref[0, 0].astype(BF16)
        wuc[...] = wu_ref[0, 0].astype(BF16)

    @pl.when(b < nv_ref[0])
    def _():
        x = x_ref[...].astype(BF16)
        gate = jnp.dot(x, wgc[...], preferred_element_type=F32)
        up = jnp.dot(x, wuc[...], preferred_element_type=F32)
        o_ref[...] = (gate * _sigmoid(gate) * up).astype(BF16)

    @pl.when(b >= nv_ref[0])
    def _():
        o_ref[...] = jnp.zeros_like(o_ref)


def _down_kernel(be_ref, nv_ref, a_ref, wd_ref, o_ref, wdc):
    b = pl.program_id(1)

    @pl.when(_expert_changed(be_ref, b))
    def _():
        wdc[...] = wd_ref[0, 0].astype(BF16)

    @pl.when(b < nv_ref[0])
    def _():
        o_ref[...] = jnp.dot(a_ref[...], wdc[...], preferred_element_type=F32)

    @pl.when(b >= nv_ref[0])
    def _():
        o_ref[...] = jnp.zeros_like(o_ref)


def _moe_matmuls(l, xs, block_e, n_valid, w_gate, w_up, w_down):
    R = xs.shape[0]
    nb = R // MOE_BLOCK
    ce = 512
    live = lambda b, nv: jnp.minimum(b, nv[0] - 1)
    act = pl.pallas_call(
        _gateup_kernel,
        grid_spec=pltpu.PrefetchScalarGridSpec(
            num_scalar_prefetch=2,
            grid=(D_EXPERT // ce, nb),
            in_specs=[pl.BlockSpec((MOE_BLOCK, D_MODEL), lambda c, b, be, nv: (live(b, nv), 0)),
                      pl.BlockSpec((1, 1, D_MODEL, ce), lambda c, b, be, nv: (l, be[b], 0, c)),
                      pl.BlockSpec((1, 1, D_MODEL, ce), lambda c, b, be, nv: (l, be[b], 0, c))],
            out_specs=pl.BlockSpec((MOE_BLOCK, ce), lambda c, b, be, nv: (b, c)),
            scratch_shapes=[pltpu.VMEM((D_MODEL, ce), BF16), pltpu.VMEM((D_MODEL, ce), BF16)]),
        out_shape=jax.ShapeDtypeStruct((R, D_EXPERT), BF16),
        compiler_params=_params("arbitrary", "arbitrary"),
        name="moe_gate_up",
    )(block_e, n_valid, xs, w_gate, w_up)
    cd = 1024
    return pl.pallas_call(
        _down_kernel,
        grid_spec=pltpu.PrefetchScalarGridSpec(
            num_scalar_prefetch=2,
            grid=(D_MODEL // cd, nb),
            in_specs=[pl.BlockSpec((MOE_BLOCK, D_EXPERT), lambda c, b, be, nv: (live(b, nv), 0)),
                      pl.BlockSpec((1, 1, D_EXPERT, cd), lambda c, b, be, nv: (l, be[b], 0, c))],
            out_specs=pl.BlockSpec((MOE_BLOCK, cd), lambda c, b, be, nv: (b, c)),
            scratch_shapes=[pltpu.VMEM((D_EXPERT, cd), BF16)]),
        out_shape=jax.ShapeDtypeStruct((R, D_MODEL), F32),
        compiler_params=_params("arbitrary", "arbitrary"),
        name="moe_down",
    )(block_e, n_valid, act, w_down)


def _combine_kernel(final, pos_ref, x_ref, w0_ref, w1_ref, gt_ref, fg_ref, yb_ref, o_ref, y0_buf, y1_buf, sem):
    bb, tl, _ = x_ref.shape
    tile = pl.program_id(0) * pl.num_programs(1) + pl.program_id(1)
    base = tile * (TOP_K * ROW_TILE)

    def row_copy(src_row, buf, r):
        return pltpu.make_async_copy(yb_ref.at[pl.ds(src_row, 1), :], buf.at[pl.ds(r, 1), :], sem)

    def issue(r, carry):
        row_copy(pos_ref[base + TOP_K * r], y0_buf, r).start()
        row_copy(pos_ref[base + TOP_K * r + 1], y1_buf, r).start()
        return carry

    def drain(r, carry):
        row_copy(0, y0_buf, 0).wait()
        row_copy(0, y1_buf, 0).wait()
        return carry

    lax.fori_loop(0, ROW_TILE, issue, 0)
    lax.fori_loop(0, ROW_TILE, drain, 0)

    w0 = w0_ref[...]
    w1 = w1_ref[...]
    for c in range(D_MODEL // LANES):
        cols = slice(c * LANES, (c + 1) * LANES)
        ff = y0_buf[:, cols] * w0 + y1_buf[:, cols] * w1
        o_ref[:, :, cols] = x_ref[:, :, cols] + gt_ref[:, :, cols] * ff.reshape(bb, tl, LANES)
    if final:
        x2 = o_ref[...]
        ms = jnp.mean(x2 * x2, axis=-1, keepdims=True)
        o_ref[...] = x2 * lax.rsqrt(ms + RMS_EPS) * fg_ref[...]


def _combine(x1, yb, pos, w0b, w1b, mod, l, final_g, final):
    B, L, _ = x1.shape
    bb, tl = _tiles(B, L)
    nl = L // tl
    xs = pl.BlockSpec((bb, tl, D_MODEL), lambda b, i, *_: (b, i, 0))
    rows = pl.BlockSpec((bb * tl, LANES), lambda b, i, *_: (b * nl + i, 0))
    return pl.pallas_call(
        functools.partial(_combine_kernel, final),
        grid_spec=pltpu.PrefetchScalarGridSpec(
            num_scalar_prefetch=1,
            grid=(B // bb, nl),
            in_specs=[xs, rows, rows, _mod_spec(bb, l, 5),
                      pl.BlockSpec((1, D_MODEL), lambda b, i, *_: (0, 0)),
                      pl.BlockSpec(memory_space=pl.ANY)],
            out_specs=xs,
            scratch_shapes=[pltpu.VMEM((ROW_TILE, D_MODEL), F32), pltpu.VMEM((ROW_TILE, D_MODEL), F32),
                            pltpu.SemaphoreType.DMA(())]),
        out_shape=jax.ShapeDtypeStruct((B, L, D_MODEL), F32),
        compiler_params=_params("arbitrary", "arbitrary"),
        name="moe_combine",
    )(pos, x1, w0b, w1b, mod, final_g, yb)


def _vh(x, axis=-1):
    return jnp.take(x, jnp.asarray(VH_PERM), axis=axis)


def _layer_weights(l, w_in, w_pool, pool_scale, mu_shift, w0, w2, a0, a2, g2, k_k, k_a, r_k,
                   lnx_w, lnx_b, w_out, norm1_g, norm2_g):
    W = RWKV_WIDTH
    wz = w_in[l][:, POOL_WIDTH:]
    wz = jnp.concatenate([wz[:, :2 * W], _vh(wz[:, 2 * W:3 * W]), wz[:, 3 * W:],
                          jnp.zeros((D_MODEL, Z_WIDTH - SHIFT_WIDTH), F32)], axis=1).astype(BF16)
    mu = mu_shift[l]
    mu = jnp.concatenate([mu[:2 * W], _vh(mu[2 * W:3 * W]), mu[3 * W:],
                          jnp.zeros((Z_WIDTH - SHIFT_WIDTH,), F32)]).reshape(1, Z_WIDTH)
    zpad = lambda m, top, tot: jnp.concatenate(
        [jnp.zeros((top, W), F32), m, jnp.zeros((tot - top - m.shape[0], W), F32)], axis=0).astype(BF16)
    wo = w_out[l]
    wo = jnp.concatenate([wo[:POOL_WIDTH], _vh(wo[POOL_WIDTH:], axis=0)], axis=0).astype(BF16)
    r1 = lambda x: x.reshape(1, -1)
    return dict(
        wp=w_in[l][:, :POOL_WIDTH].astype(BF16), wz=wz, mu=mu,
        w_pool=w_pool[l].astype(BF16), pool_scale=r1(pool_scale[l]),
        w0=r1(w0[l]), w2=zpad(w2[l], 0, LANES), a0=r1(a0[l]), a2=zpad(a2[l], DECAY_LORA, LANES),
        g2=zpad(_vh(g2[l]), 0, Z_WIDTH - GATE_OFF),
        k_k=r1(k_k[l]), k_a=r1(k_a[l]), r_k=r1(r_k[l]),
        lnx_w=r1(_vh(lnx_w[l])), lnx_b=r1(_vh(lnx_b[l])), w_out=wo,
        norm1_g=r1(norm1_g[l]), norm2_g=r1(norm2_g[l]))


def _shift_to_z(s):
    W = RWKV_WIDTH
    out = jnp.concatenate([s[:, :2 * W], _vh(s[:, 2 * W:3 * W]), s[:, 3 * W:],
                           jnp.zeros((s.shape[0], Z_WIDTH - SHIFT_WIDTH), s.dtype)], axis=1)
    return out[:, None, :]


def _z_to_shift(zrow):
    W = RWKV_WIDTH
    v = jnp.take(zrow[:, 2 * W:3 * W], jnp.asarray(VH_INV), axis=1)
    return jnp.concatenate([zrow[:, :2 * W], v, zrow[:, 3 * W:SHIFT_WIDTH]], axis=1)


def _state_to_tile(s):
    B = s.shape[0]
    return s.transpose(0, 3, 2, 1).reshape(B, HEAD_DIM, RWKV_WIDTH)


def _tile_to_state(p):
    B = p.shape[0]
    return p.reshape(B, HEAD_DIM, HEAD_DIM, N_HEADS).transpose(0, 3, 2, 1)


def _mix_half(l, x, mod, st_wkv, st_pool, st_shift, pos0, lw, w_router_p):
    B, L, _ = x.shape
    p, z = _inproj(x, lw["norm1_g"], mod, l, lw["wp"], lw["wz"])
    st16 = jnp.concatenate([jnp.zeros((B, 1, POOL_WIDTH), F32), st_pool], axis=1)
    ypool, buf = _pool(p, st16, lw["w_pool"], lw["pool_scale"], pos0)
    aq, wb, kk, v, br, kr, bs, g = _rwkv_pre(z, _shift_to_z(st_shift), lw)
    y, s_fin = _scan((aq, wb, kk, v, br, kr), _state_to_tile(st_wkv), B, L)
    x1, h2, scores = _post(y, v, g, bs, ypool, x, lw, mod, l, w_router_p)
    return x1, h2, scores, _tile_to_state(s_fin), buf, _z_to_shift(z[:, -1])


def kernel(x_prompt, x_sample, state_wkv, state_pool, state_shift, c_prompt, c_sample, w_ada, b_ada, norm1_g, norm2_g, w_in, w_pool, pool_scale, mu_shift, w0, w2, a0, a2, g2, k_k, k_a, r_k, lnx_w, lnx_b, w_out, w_router, b_router, w_gate, w_up, w_down, final_g):
    Bp, Lp, _ = x_prompt.shape
    Bs, Ls, _ = x_sample.shape
    Tp, Ts = Bp * Lp, Bs * Ls
    mod = _ada(jnp.concatenate([c_prompt, c_sample], axis=0), w_ada, b_ada)
    mod = mod.reshape(DEPTH, Bp + Bs, 6, 1, D_MODEL).transpose(0, 2, 1, 3, 4)
    mod_p, mod_s = mod[:, :, :Bp], mod[:, :, Bp:]
    w_router_p = jnp.concatenate([w_router, jnp.zeros((D_MODEL, LANES - N_EXPERTS), F32)], axis=1)
    fg = final_g.reshape(1, D_MODEL)
    n_rows = _moe_rows(Tp + Ts) * MOE_BLOCK

    xp, xs_ = x_prompt, x_sample
    outs_p, outs_s = [], []
    for l in range(DEPTH):
        lw = _layer_weights(l, w_in, w_pool, pool_scale, mu_shift, w0, w2, a0, a2, g2, k_k, k_a, r_k,
                            lnx_w, lnx_b, w_out, norm1_g, norm2_g)
        zp = lambda *s: jnp.zeros(s, F32)
        x1p, h2p, scp, wkv_p, buf_p, shf_p = _mix_half(
            l, xp, mod_p, zp(Bp, N_HEADS, HEAD_DIM, HEAD_DIM), zp(Bp, POOL_BUF, POOL_WIDTH),
            zp(Bp, SHIFT_WIDTH), 0, lw, w_router_p)
        x1s, h2s, scs, wkv_s, buf_s, shf_s = _mix_half(
            l, xs_, mod_s, state_wkv[l], state_pool[l], state_shift[l], PAST_LEN, lw, w_router_p)
        outs_p.append((wkv_p, buf_p, shf_p))
        outs_s.append((wkv_s, buf_s, shf_s))

        scores = jnp.concatenate([scp, scs], axis=0)[:, :N_EXPERTS]
        idx, wts = _route(scores, b_router)
        dest, block_e, n_valid, pad_start, pad_cnt = _dispatch_plan(idx)
        xs_rows = _dispatch_rows(h2p, h2s, dest, pad_start, pad_cnt, n_rows)
        yb = _moe_matmuls(l, xs_rows, block_e, n_valid, w_gate, w_up, w_down)
        w0b = jnp.broadcast_to(wts[:, 0:1], (Tp + Ts, LANES))
        w1b = jnp.broadcast_to(wts[:, 1:2], (Tp + Ts, LANES))
        final = l == DEPTH - 1
        xp = _combine(x1p, yb, dest[:TOP_K * Tp], w0b[:Tp], w1b[:Tp], mod_p, l, fg, final)
        xs_ = _combine(x1s, yb, dest[TOP_K * Tp:], w0b[Tp:], w1b[Tp:], mod_s, l, fg, final)

    st = lambda outs, i: jnp.stack([o[i] for o in outs])
    return (xp, xs_, st(outs_p, 0), st(outs_p, 1), st(outs_p, 2),
            st(outs_s, 0), st(outs_s, 1), st(outs_s, 2))
```

```python
import functools

import numpy as np
import jax
import jax.numpy as jnp
from jax import lax
from jax.experimental import pallas as pl
from jax.experimental.pallas import tpu as pltpu

F32 = jnp.float32
BF16 = jnp.bfloat16

D_MODEL = 2048
DEPTH = 4
PAST_LEN = 16384
POOL_WIDTH = 1024
POOL_WINDOWS = (2, 4, 8, 16)
POOL_GROUP = 256
POOL_BUF = 15
POOL_HALO = 16
RWKV_WIDTH = 1024
HEAD_DIM = 64
N_HEADS = 16
DECAY_LORA = 64
AAA_LORA = 64
GATE_LORA = 160
SHIFT_WIDTH = 3 * RWKV_WIDTH + DECAY_LORA + AAA_LORA + GATE_LORA
Z_WIDTH = 3456
LORA_OFF = 3 * RWKV_WIDTH
GATE_OFF = LORA_OFF + 128
LNX_EPS = 64e-5
RMS_EPS = 1e-6
N_EXPERTS = 16
N_EXPERT_GROUPS = 4
EXPERTS_PER_GROUP = 4
TOP_K = 2
D_EXPERT = 1024

LANES = 128
SUBLANES = 8
ROW_TILE = 256
MOE_BLOCK = 512
VMEM_LIMIT = 56 * 1024 * 1024
HIGHEST = lax.Precision.HIGHEST

_P = np.arange(RWKV_WIDTH)
VH_PERM = (_P % N_HEADS) * HEAD_DIM + _P // N_HEADS
_C = np.arange(RWKV_WIDTH)
VH_INV = (_C % HEAD_DIM) * N_HEADS + _C // HEAD_DIM


def _params(*sem, **kw):
    return pltpu.CompilerParams(dimension_semantics=sem, vmem_limit_bytes=VMEM_LIMIT, **kw)


def _sigmoid(x):
    return 1.0 / (1.0 + jnp.exp(-x))


def _tiles(B, L):
    if L >= ROW_TILE:
        assert L % ROW_TILE == 0
        return 1, ROW_TILE
    assert ROW_TILE % L == 0 and L % SUBLANES == 0 and B % (ROW_TILE // L) == 0
    return ROW_TILE // L, L


def _mod_spec(bb, l, k):
    return pl.BlockSpec((None, None, bb, 1, D_MODEL), lambda b, i, *_: (l, k, b, 0, 0))


def _ada_kernel(c_ref, w_ref, b_ref, o_ref):
    c = c_ref[...]
    s = (c * _sigmoid(c)).astype(BF16)
    o_ref[0] = jnp.dot(s, w_ref[0].astype(BF16), preferred_element_type=F32) + b_ref[0]


def _ada(c_all, w_ada, b_ada):
    bc = c_all.shape[0]
    tn = 512
    return pl.pallas_call(
        _ada_kernel,
        grid=(DEPTH, 6 * D_MODEL // tn),
        in_specs=[pl.BlockSpec((bc, D_MODEL), lambda l, n: (0, 0)),
                  pl.BlockSpec((1, D_MODEL, tn), lambda l, n: (l, 0, n)),
                  pl.BlockSpec((1, 1, tn), lambda l, n: (l, 0, n))],
        out_specs=pl.BlockSpec((1, bc, tn), lambda l, n: (l, 0, n)),
        out_shape=jax.ShapeDtypeStruct((DEPTH, bc, 6 * D_MODEL), F32),
        compiler_params=_params("arbitrary", "arbitrary"),
        name="ada_mod",
    )(c_all, w_ada, b_ada.reshape(DEPTH, 1, 6 * D_MODEL))


def _ada_norm_val(x, g, shift, scale):
    ms = jnp.mean(x * x, axis=-1, keepdims=True)
    return x * lax.rsqrt(ms + RMS_EPS) * g * (1.0 + scale) + shift


def _inproj_kernel(x_ref, g_ref, sh_ref, sc_ref, wp_ref, wz_ref, p_ref, z_ref):
    bb, tl, _ = x_ref.shape
    h = _ada_norm_val(x_ref[...], g_ref[...], sh_ref[...], sc_ref[...])
    h = h.reshape(bb * tl, D_MODEL).astype(BF16)
    p_ref[...] = jnp.dot(h, wp_ref[...], preferred_element_type=F32).reshape(bb, tl, POOL_WIDTH)
    step = 1152
    for n0 in range(0, Z_WIDTH, step):
        z_ref[:, :, n0:n0 + step] = jnp.dot(
            h, wz_ref[:, n0:n0 + step], preferred_element_type=F32).reshape(bb, tl, step)


def _inproj(x, g, mod, l, wp, wz):
    B, L, _ = x.shape
    bb, tl = _tiles(B, L)
    once = pl.Buffered(1)
    return pl.pallas_call(
        _inproj_kernel,
        grid=(B // bb, L // tl),
        in_specs=[pl.BlockSpec((bb, tl, D_MODEL), lambda b, i: (b, i, 0)),
                  pl.BlockSpec((1, D_MODEL), lambda b, i: (0, 0)),
                  _mod_spec(bb, l, 0), _mod_spec(bb, l, 1),
                  pl.BlockSpec((D_MODEL, POOL_WIDTH), lambda b, i: (0, 0), pipeline_mode=once),
                  pl.BlockSpec((D_MODEL, Z_WIDTH), lambda b, i: (0, 0), pipeline_mode=once)],
        out_specs=[pl.BlockSpec((bb, tl, POOL_WIDTH), lambda b, i: (b, i, 0)),
                   pl.BlockSpec((bb, tl, Z_WIDTH), lambda b, i: (b, i, 0))],
        out_shape=[jax.ShapeDtypeStruct((B, L, POOL_WIDTH), F32),
                   jax.ShapeDtypeStruct((B, L, Z_WIDTH), F32)],
        compiler_params=_params("arbitrary", "arbitrary"),
        name="norm_inproj",
    )(x, g, mod, mod, wp, wz)


def _pool_kernel(pos0, p_ref, halo_ref, st_ref, w_ref, scale_ref, y_ref, buf_ref, ext_ref):
    bb, tl, _ = p_ref.shape
    i = pl.program_id(1)

    @pl.when(i == 0)
    def _():
        ext_ref[:, 0:POOL_HALO, :] = st_ref[...]

    @pl.when(i > 0)
    def _():
        ext_ref[:, 0:POOL_HALO, :] = halo_ref[...]

    ext_ref[:, POOL_HALO:POOL_HALO + tl, :] = p_ref[...]
    pos = pos0 + i * tl + lax.broadcasted_iota(jnp.int32, (1, tl, POOL_GROUP), 1)
    for gi, win in enumerate(POOL_WINDOWS):
        c0 = gi * POOL_GROUP
        cur = p_ref[:, :, c0:c0 + POOL_GROUP]
        acc = cur
        for s in range(1, win):
            acc = acc + ext_ref[:, POOL_HALO - s:POOL_HALO - s + tl, c0:c0 + POOL_GROUP]
        cnt = jnp.minimum(pos + 1, win).astype(F32)
        pooled = (acc / cnt - cur).reshape(bb * tl, POOL_GROUP).astype(BF16)
        out = jnp.dot(pooled, w_ref[gi], preferred_element_type=F32) * scale_ref[:, c0:c0 + POOL_GROUP]
        y_ref[:, c0:c0 + POOL_GROUP] = out.astype(BF16)

    @pl.when(i == pl.num_programs(1) - 1)
    def _():
        buf_ref[...] = ext_ref[:, tl + 1:tl + POOL_HALO, :]


def _pool(p, st16, w_pool, scale, pos0):
    B, L, _ = p.shape
    bb, tl = _tiles(B, L)
    nl = L // tl
    if L >= POOL_HALO:
        halo, hstep = p, tl // POOL_HALO
        halo_spec = pl.BlockSpec((bb, POOL_HALO, POOL_WIDTH),
                                 lambda b, i: (b, jnp.maximum(i * hstep - 1, 0), 0))
    else:
        halo = st16
        halo_spec = pl.BlockSpec((bb, POOL_HALO, POOL_WIDTH), lambda b, i: (b, 0, 0))
    return pl.pallas_call(
        functools.partial(_pool_kernel, pos0),
        grid=(B // bb, nl),
        in_specs=[pl.BlockSpec((bb, tl, POOL_WIDTH), lambda b, i: (b, i, 0)),
                  halo_spec,
                  pl.BlockSpec((bb, POOL_HALO, POOL_WIDTH), lambda b, i: (b, 0, 0)),
                  pl.BlockSpec((len(POOL_WINDOWS), POOL_GROUP, POOL_GROUP), lambda b, i: (0, 0, 0)),
                  pl.BlockSpec((1, POOL_WIDTH), lambda b, i: (0, 0))],
        out_specs=[pl.BlockSpec((bb * tl, POOL_WIDTH), lambda b, i: (b * nl + i, 0)),
                   pl.BlockSpec((bb, POOL_BUF, POOL_WIDTH), lambda b, i: (b, 0, 0))],
        out_shape=[jax.ShapeDtypeStruct((B * L, POOL_WIDTH), BF16),
                   jax.ShapeDtypeStruct((B, POOL_BUF, POOL_WIDTH), F32)],
        scratch_shapes=[pltpu.VMEM((bb, POOL_HALO + tl, POOL_WIDTH), F32)],
        compiler_params=_params("arbitrary", "arbitrary"),
        name="pool_mixer",
    )(p, halo, st16, w_pool, scale)


def _seg_mats():
    ch = np.arange(RWKV_WIDTH) // HEAD_DIM
    ln = np.arange(LANES)
    seg = (ch[:, None] == (ln[None, :] % N_HEADS)).astype(np.float32)
    exp = ((ln[:, None] == ch[None, :])).astype(np.float32)
    return jnp.asarray(seg), jnp.asarray(exp)


def _pre_kernel(z_ref, zp_ref, st_ref, mu_ref, w0_ref, w2_ref, a0_ref, a2_ref, g2_ref,
                kk_ref, ka_ref, rk_ref, seg_ref, exp_ref,
                aq_o, wb_o, kk_o, v_o, br_o, kr_o, bs_o, g_o):
    bb, tl, _ = z_ref.shape
    i = pl.program_id(1)
    z = z_ref[...]
    first = jnp.where(i == 0, st_ref[...], zp_ref[:, SUBLANES - 1:SUBLANES, :])
    tpos = lax.broadcasted_iota(jnp.int32, (1, tl, LANES), 1)
    n = bb * tl

    def shifted(c0, c1):
        zc = z[:, :, c0:c1]
        prev = pltpu.roll(zc, 1, axis=1)
        reps = (c1 - c0) // LANES
        t0 = jnp.concatenate([tpos] * reps, axis=2) if reps > 1 else tpos
        prev = jnp.where(t0 == 0, first[:, :, c0:c1], prev)
        return (zc + (prev - zc) * mu_ref[:, c0:c1]).reshape(n, c1 - c0)

    W = RWKV_WIDTH
    r = shifted(0, W)
    k = shifted(W, 2 * W)
    v = shifted(2 * W, 3 * W)
    lora = shifted(LORA_OFF, LORA_OFF + LANES)
    gd = shifted(GATE_OFF, Z_WIDTH)

    w_lin = w0_ref[...] + jnp.dot(jnp.tanh(lora).astype(BF16), w2_ref[...], preferred_element_type=F32)
    a_lin = a0_ref[...] + jnp.dot(lora.astype(BF16), a2_ref[...], preferred_element_type=F32)
    g_o[...] = jnp.dot(_sigmoid(gd).astype(BF16), g2_ref[...], preferred_element_type=F32)
    w = -(jnp.maximum(-w_lin, 0.0) + jnp.log(1.0 + jnp.exp(-jnp.abs(w_lin)))) - 0.5
    decay = jnp.exp(-jnp.exp(w))
    a_sig = _sigmoid(a_lin)

    seg = seg_ref[...]

    def head_sum(x):
        return jnp.dot(x, seg, precision=HIGHEST, preferred_element_type=F32)

    kk = k * kk_ref[...]
    inv = 1.0 / jnp.maximum(jnp.sqrt(head_sum(kk * kk)), 1e-12)
    kk = kk * jnp.dot(inv, exp_ref[...], precision=HIGHEST, preferred_element_type=F32)
    k2 = k * (1.0 + (a_sig - 1.0) * ka_ref[...])
    b = kk * a_sig

    low_half = lax.broadcasted_iota(jnp.int32, (n, LANES), 1) < HEAD_DIM

    def store_pair(o_ref, xa, xb):
        for c in range(RWKV_WIDTH // LANES):
            cols = slice(c * LANES, (c + 1) * LANES)
            ac, bc = xa[:, cols], xb[:, cols]
            a_sw = pltpu.roll(ac, HEAD_DIM, axis=1)
            b_sw = pltpu.roll(bc, HEAD_DIM, axis=1)
            o_ref[pl.ds(2 * c, n, stride=N_HEADS), :] = jnp.where(low_half, ac, b_sw)
            o_ref[pl.ds(2 * c + 1, n, stride=N_HEADS), :] = jnp.where(low_half, a_sw, bc)

    store_pair(aq_o, -kk, decay * r)
    store_pair(wb_o, decay, b)
    store_pair(kk_o, k2, k2)
    for c in range(RWKV_WIDTH // LANES):
        v_o[pl.ds(c, n, stride=SUBLANES), :] = v[:, c * LANES:(c + 1) * LANES]
    br_o[...] = head_sum(b * r)
    kr_o[...] = head_sum(k2 * r)
    bs_o[...] = head_sum(r * k2 * rk_ref[...])


def _rwkv_pre(z, st_shift, lw):
    B, L, _ = z.shape
    bb, tl = _tiles(B, L)
    nl = L // tl
    pstep = tl // SUBLANES
    seg, exp = _seg_mats()
    row = lambda wdt: pl.BlockSpec((1, wdt), lambda b, i: (0, 0))
    full = lambda a: pl.BlockSpec(a.shape, lambda b, i: (0,) * a.ndim)
    out_w = pl.BlockSpec((bb * tl, RWKV_WIDTH), lambda b, i: (b * nl + i, 0))
    out_s = pl.BlockSpec((bb * tl, LANES), lambda b, i: (b * nl + i, 0))
    out_p = pl.BlockSpec((bb * tl * N_HEADS, LANES), lambda b, i: (b * nl + i, 0))
    out_v = pl.BlockSpec((bb * tl * SUBLANES, LANES), lambda b, i: (b * nl + i, 0))
    T = B * L
    wide = jax.ShapeDtypeStruct((T, RWKV_WIDTH), F32)
    pair = jax.ShapeDtypeStruct((T * N_HEADS, LANES), F32)
    folded = jax.ShapeDtypeStruct((T * SUBLANES, LANES), F32)
    small = jax.ShapeDtypeStruct((T, LANES), F32)
    return pl.pallas_call(
        _pre_kernel,
        grid=(B // bb, nl),
        in_specs=[pl.BlockSpec((bb, tl, Z_WIDTH), lambda b, i: (b, i, 0)),
                  pl.BlockSpec((bb, SUBLANES, Z_WIDTH), lambda b, i: (b, jnp.maximum(i * pstep - 1, 0), 0)),
                  pl.BlockSpec((bb, 1, Z_WIDTH), lambda b, i: (b, 0, 0)),
                  row(Z_WIDTH), row(RWKV_WIDTH), full(lw["w2"]), row(RWKV_WIDTH), full(lw["a2"]),
                  full(lw["g2"]), row(RWKV_WIDTH), row(RWKV_WIDTH), row(RWKV_WIDTH), full(seg), full(exp)],
        out_specs=[out_p] * 3 + [out_v] + [out_s] * 3 + [out_w],
        out_shape=[pair] * 3 + [folded] + [small] * 3 + [wide],
        compiler_params=_params("arbitrary", "arbitrary"),
        name="rwkv_pre",
    )(z, z, st_shift, lw["mu"], lw["w0"], lw["w2"], lw["a0"], lw["a2"], lw["g2"],
      lw["k_k"], lw["k_a"], lw["r_k"], seg, exp)


N_ACC = 4


def _scan_kernel(aq_ref, wb_ref, kk_ref, v_ref, br_ref, kr_ref, s0_ref, y_ref, s_ref, op_tiles, op_first):
    tc = v_ref.shape[0]
    rows = SUBLANES * N_HEADS
    n_groups = tc // SUBLANES

    @pl.when(pl.program_id(1) == 0)
    def _():
        s_ref[...] = s0_ref[...]

    def keys_to_sublanes(blk):
        return jnp.concatenate([blk] * SUBLANES, axis=0).T

    def put_pair(op_ref, slot, first, tile):
        op_ref[slot, first] = tile[0:HEAD_DIM]
        op_ref[slot, first + 1] = tile[HEAD_DIM:]

    def tree_sum(parts):
        while len(parts) > 1:
            parts = [parts[i] + parts[i + 1] for i in range(0, len(parts), 2)]
        return parts[0]

    def reduce_keys(op_ref, slot, state_of):
        sa_p = [None] * N_ACC
        yq_p = [None] * N_ACC
        for j in range(HEAD_DIM):
            sj = state_of(j)
            pa = sj * op_ref[slot, 0, pl.ds(j, 1), :]
            pq = sj * op_ref[slot, 1, pl.ds(j, 1), :]
            i = j % N_ACC
            sa_p[i] = pa if sa_p[i] is None else sa_p[i] + pa
            yq_p[i] = pq if yq_p[i] is None else yq_p[i] + pq
        return tree_sum(sa_p), tree_sum(yq_p)

    def prepare(g, op_ref):
        r0 = pl.multiple_of(jnp.minimum(g, n_groups - 1) * rows, rows)
        r_next = pl.multiple_of(jnp.minimum(g + 1, n_groups - 1) * rows, rows)
        aq = aq_ref[pl.ds(r0, rows), :]
        wb = wb_ref[pl.ds(r0, rows), :]
        kx = kk_ref[pl.ds(r0, rows), :]
        aq_next = aq_ref[pl.ds(r_next, N_HEADS), :]
        for t in range(SUBLANES):
            nxt = aq[(t + 1) * N_HEADS:(t + 2) * N_HEADS] if t + 1 < SUBLANES else aq_next
            put_pair(op_ref, t, 0, keys_to_sublanes(nxt))
            put_pair(op_ref, t, 2, keys_to_sublanes(wb[t * N_HEADS:(t + 1) * N_HEADS]))
            op_ref[t, 4] = keys_to_sublanes(kx[t * N_HEADS:(t + 1) * N_HEADS])[0:HEAD_DIM]

    def run_group(g, carry, op_ref):
        sa, yq = carry
        for t in range(SUBLANES):
            step = g * SUBLANES + t
            v = v_ref[step]
            y_ref[step] = yq + sa * br_ref[pl.ds(step, 1), :] + v * kr_ref[pl.ds(step, 1), :]

            def updated(j, t=t, sa=sa, v=v):
                sj = (s_ref[0, j] * op_ref[t, 2, pl.ds(j, 1), :] + sa * op_ref[t, 3, pl.ds(j, 1), :]
                      + v * op_ref[t, 4, pl.ds(j, 1), :])
                s_ref[0, j] = sj
                return sj

            sa, yq = reduce_keys(op_ref, t, updated)
        return sa, yq

    put_pair(op_first, 0, 0, keys_to_sublanes(aq_ref[0:N_HEADS, :]))
    first = reduce_keys(op_first, 0, lambda j: s_ref[0, j])

    def one_group(g, carry):
        prepare(g, op_tiles)
        return run_group(g, carry, op_tiles)

    lax.fori_loop(0, n_groups, one_group, first)


def _scan(ops, s0, B, L):
    aq, wb, kk, v, br, kr = ops
    tc = min(L, 128)
    nc = L // tc
    T = B * L
    nblk = RWKV_WIDTH // LANES
    op_spec = pl.BlockSpec((tc * N_HEADS, LANES), lambda bi, c: (bi * nc + c, 0))
    v_spec = pl.BlockSpec((tc, nblk, LANES), lambda bi, c: (bi * nc + c, 0, 0))
    s_spec = pl.BlockSpec((tc, LANES), lambda bi, c: (bi * nc + c, 0))
    st_spec = pl.BlockSpec((1, HEAD_DIM, nblk, LANES), lambda bi, c: (bi, 0, 0, 0))
    y, s = pl.pallas_call(
        _scan_kernel,
        grid=(B, nc),
        in_specs=[op_spec] * 3 + [v_spec, s_spec, s_spec, st_spec],
        out_specs=[v_spec, st_spec],
        out_shape=[jax.ShapeDtypeStruct((T, nblk, LANES), F32),
                   jax.ShapeDtypeStruct((B, HEAD_DIM, nblk, LANES), F32)],
        scratch_shapes=[pltpu.VMEM((SUBLANES, 5, HEAD_DIM, LANES), F32),
                        pltpu.VMEM((1, 2, HEAD_DIM, LANES), F32)],
        compiler_params=_params("arbitrary", "arbitrary"),
        name="rwkv_scan",
    )(aq, wb, kk, v.reshape(T, nblk, LANES), br, kr, s0)
    return y.reshape(T * nblk, LANES), s


def _post_kernel(y_ref, v_ref, g_ref, bs_ref, yp_ref, x_ref, lw_ref, lb_ref, wo_ref,
                 gta_ref, n2_ref, sh_ref, sc_ref, wr_ref, x1_ref, h2_ref, s_ref):
    bb, tl, _ = x_ref.shape
    nblk = RWKV_WIDTH // LANES
    n = bb * tl
    cols = [slice(c * LANES, (c + 1) * LANES) for c in range(nblk)]
    folded = lambda ref, c: ref[pl.ds(c, n, stride=nblk), :]
    y = [folded(y_ref, c) for c in range(nblk)]

    def head_total(parts):
        s = parts[0]
        for part in parts[1:]:
            s = s + part
        for sh in (N_HEADS, 2 * N_HEADS, 4 * N_HEADS):
            s = s + pltpu.roll(s, sh, axis=1)
        return s

    mean = head_total(y) * (1.0 / HEAD_DIM)
    dev = [yc - mean for yc in y]
    var = head_total([d * d for d in dev]) * (1.0 / HEAD_DIM)
    rstd = lax.rsqrt(var + LNX_EPS)
    bs = bs_ref[...]
    outs = []
    for ci, c in enumerate(cols):
        yn = dev[ci] * rstd * lw_ref[:, c] + lb_ref[:, c]
        outs.append(((yn + bs * folded(v_ref, ci)) * g_ref[:, c]).astype(BF16))
    yr = jnp.concatenate(outs, axis=1)
    mix = jnp.dot(yp_ref[...], wo_ref[0:POOL_WIDTH, :], preferred_element_type=F32)
    mix = mix + jnp.dot(yr, wo_ref[POOL_WIDTH:, :], preferred_element_type=F32)
    x1 = x_ref[...] + gta_ref[...] * mix.reshape(bb, tl, D_MODEL)
    x1_ref[...] = x1
    h2 = _ada_norm_val(x1, n2_ref[...], sh_ref[...], sc_ref[...]).reshape(bb * tl, D_MODEL)
    h2_ref[...] = h2
    s_ref[...] = _sigmoid(jnp.dot(h2, wr_ref[...], precision=HIGHEST, preferred_element_type=F32))


def _post(y, v, g, bs, ypool, x, lw, mod, l, w_router_p):
    B, L, _ = x.shape
    bb, tl = _tiles(B, L)
    nl = L // tl
    T = B * L
    rows = lambda wdt: pl.BlockSpec((bb * tl, wdt), lambda b, i: (b * nl + i, 0))
    fold_rows = pl.BlockSpec((bb * tl * RWKV_WIDTH // LANES, LANES), lambda b, i: (b * nl + i, 0))
    row = lambda wdt: pl.BlockSpec((1, wdt), lambda b, i: (0, 0))
    xs = pl.BlockSpec((bb, tl, D_MODEL), lambda b, i: (b, i, 0))
    return pl.pallas_call(
        _post_kernel,
        grid=(B // bb, nl),
        in_specs=[fold_rows, fold_rows, rows(RWKV_WIDTH), rows(LANES), rows(POOL_WIDTH), xs,
                  row(RWKV_WIDTH), row(RWKV_WIDTH),
                  pl.BlockSpec((D_MODEL, D_MODEL), lambda b, i: (0, 0)),
                  _mod_spec(bb, l, 2), row(D_MODEL), _mod_spec(bb, l, 3), _mod_spec(bb, l, 4),
                  pl.BlockSpec((D_MODEL, LANES), lambda b, i: (0, 0))],
        out_specs=[xs, rows(D_MODEL), rows(LANES)],
        out_shape=[jax.ShapeDtypeStruct((B, L, D_MODEL), F32),
                   jax.ShapeDtypeStruct((T, D_MODEL), F32),
                   jax.ShapeDtypeStruct((T, LANES), F32)],
        compiler_params=_params("arbitrary", "arbitrary"),
        name="post_outproj_norm2",
    )(y, v, g, bs, ypool, x, lw["lnx_w"], lw["lnx_b"], lw["w_out"], mod, lw["norm2_g"], mod, mod, w_router_p)


def _route(scores, b_router):
    T = scores.shape[0]
    sel = scores + b_router.astype(F32)
    neg = jnp.float32(-jnp.inf)

    def top2(x):
        lane = lax.broadcasted_iota(jnp.int32, x.shape, x.ndim - 1)
        i0 = jnp.argmax(x, -1)
        m0 = jnp.max(x, -1)
        rest = jnp.where(lane == i0[..., None], neg, x)
        return m0, jnp.max(rest, -1), i0, jnp.argmax(rest, -1)

    g0, g1, _, _ = top2(sel.reshape(T, N_EXPERT_GROUPS, EXPERTS_PER_GROUP))
    gbest = jnp.argmax(g0 + g1, -1)
    in_group = (jnp.arange(N_EXPERTS) // EXPERTS_PER_GROUP)[None, :] == gbest[:, None]
    _, _, i0, i1 = top2(jnp.where(in_group, sel, neg))
    idx = jnp.stack([i0, i1], axis=1).astype(jnp.int32)
    wts = jnp.take_along_axis(scores, idx, 1)
    wts = wts / jnp.sum(wts, -1, keepdims=True)
    return idx, wts


def _moe_rows(T):
    return -(-(T * TOP_K + N_EXPERTS * (MOE_BLOCK - 1)) // MOE_BLOCK)


def _dispatch_plan(idx):
    T = idx.shape[0]
    A = T * TOP_K
    nb = _moe_rows(T)
    e_flat = idx.reshape(A)
    experts = jnp.arange(N_EXPERTS, dtype=jnp.int32)
    onehot = (e_flat[:, None] == experts[None, :]).astype(jnp.int32)
    csum = jnp.cumsum(onehot, axis=0)
    rank = jnp.sum(csum * onehot, axis=1) - 1
    counts = csum[-1]
    padded = (counts + MOE_BLOCK - 1) // MOE_BLOCK * MOE_BLOCK
    pend = jnp.cumsum(padded)
    pstart = pend - padded
    dest = (jnp.sum(onehot * pstart[None, :], axis=1) + rank).astype(jnp.int32)
    blk0 = jnp.arange(nb, dtype=jnp.int32) * MOE_BLOCK
    block_e = jnp.sum((pend[None, :] <= blk0[:, None]).astype(jnp.int32), axis=1)
    block_e = jnp.minimum(block_e, N_EXPERTS - 1).astype(jnp.int32)
    n_valid = (pend[-1] // MOE_BLOCK).astype(jnp.int32).reshape(1)
    pad_start = jnp.concatenate([pstart + counts, pend[-1:]]).astype(jnp.int32)
    pad_cnt = jnp.concatenate([padded - counts, nb * MOE_BLOCK - pend[-1:]]).astype(jnp.int32)
    return dest, block_e, n_valid, pad_start, pad_cnt


def _dispatch_kernel(tiles_a, dest_ref, pad_start_ref, pad_cnt_ref, ha_ref, hb_ref, xs_ref, sem):
    i = pl.program_id(0)
    base = i * (TOP_K * ROW_TILE)

    def row_copy(h_ref, src_row, dst_row):
        return pltpu.make_async_copy(h_ref.at[pl.ds(src_row, 1), :], xs_ref.at[pl.ds(dst_row, 1), :], sem)

    def scatter_tile(h_ref):
        def issue(r, carry):
            for k in range(TOP_K):
                row_copy(h_ref, r, dest_ref[base + TOP_K * r + k]).start(priority=k)
            return carry

        def drain(r, carry):
            for k in range(TOP_K):
                row_copy(h_ref, 0, 0).wait()
            return carry

        lax.fori_loop(0, ROW_TILE, issue, 0, unroll=8)
        lax.fori_loop(0, ROW_TILE, drain, 0, unroll=8)

    @pl.when(i < tiles_a)
    def _():
        scatter_tile(ha_ref)

    @pl.when(i >= tiles_a)
    def _():
        scatter_tile(hb_ref)

    def fill(e, wait):
        start, cnt = pad_start_ref[e], pad_cnt_ref[e]
        go = (lambda cp: cp.wait()) if wait else (lambda cp: cp.start())
        head = jnp.minimum((-start) & (SUBLANES - 1), cnt)

        def chunk(size, at):
            at = pl.multiple_of(at, SUBLANES)
            return pltpu.make_async_copy(ha_ref.at[pl.ds(0, size), :], xs_ref.at[pl.ds(at, size), :], sem)

        def one(j, carry):
            go(row_copy(ha_ref, 0, start + j))
            return carry

        lax.fori_loop(0, head, one, 0)
        pos = start + head
        rem = cnt - head

        def full(j, carry):
            go(chunk(ROW_TILE, pos + j * ROW_TILE))
            return carry

        n_full = rem // ROW_TILE
        lax.fori_loop(0, n_full, full, 0)
        pos = pos + n_full * ROW_TILE
        size = ROW_TILE // 2
        while size >= SUBLANES:
            take = (rem & size) != 0

            @pl.when(take)
            def _(size=size, pos=pos):
                go(chunk(size, pos))

            pos = pos + jnp.where(take, size, 0)
            size //= 2

    @pl.when(i == 0)
    def _():
        for e in range(N_EXPERTS + 1):
            fill(e, False)
        for e in range(N_EXPERTS + 1):
            fill(e, True)


def _dispatch_rows(ha, hb, dest, pad_start, pad_cnt, n_rows):
    tiles_a, tiles_b = ha.shape[0] // ROW_TILE, hb.shape[0] // ROW_TILE
    return pl.pallas_call(
        functools.partial(_dispatch_kernel, tiles_a),
        grid_spec=pltpu.PrefetchScalarGridSpec(
            num_scalar_prefetch=3,
            grid=(tiles_a + tiles_b,),
            in_specs=[pl.BlockSpec((ROW_TILE, D_MODEL), lambda i, *_: (jnp.minimum(i, tiles_a - 1), 0)),
                      pl.BlockSpec((ROW_TILE, D_MODEL), lambda i, *_: (jnp.maximum(i - tiles_a, 0), 0))],
            out_specs=pl.BlockSpec(memory_space=pl.ANY),
            scratch_shapes=[pltpu.SemaphoreType.DMA(())]),
        out_shape=jax.ShapeDtypeStruct((n_rows, D_MODEL), F32),
        compiler_params=_params("arbitrary"),
        name="moe_dispatch",
    )(dest, pad_start, pad_cnt, ha, hb)


def _expert_changed(be_ref, b):
    return jnp.logical_or(b == 0, be_ref[b] != be_ref[jnp.maximum(b - 1, 0)])


def _gateup_kernel(be_ref, nv_ref, x_ref, wg_ref, wu_ref, o_ref, wgc, wuc):
    b = pl.program_id(1)

    @pl.when(_expert_changed(be_ref, b))
    def _():
        wgc[...] = wg_ref[0, 0].astype(BF16)
        wuc[...] = wu_ref[0, 0].astype(BF16)

    @pl.when(b < nv_ref[0])
    def _():
        x = x_ref[...].astype(BF16)
        gate = jnp.dot(x, wgc[...], preferred_element_type=F32)
        up = jnp.dot(x, wuc[...], preferred_element_type=F32)
        o_ref[...] = (gate * _sigmoid(gate) * up).astype(BF16)

    @pl.when(b >= nv_ref[0])
    def _():
        o_ref[...] = jnp.zeros_like(o_ref)


def _down_kernel(be_ref, nv_ref, a_ref, wd_ref, o_ref, wdc):
    b = pl.program_id(1)

    @pl.when(_expert_changed(be_ref, b))
    def _():
        wdc[...] = wd_ref[0, 0].astype(BF16)

    @pl.when(b < nv_ref[0])
    def _():
        o_ref[...] = jnp.dot(a_ref[...], wdc[...], preferred_element_type=F32)

    @pl.when(b >= nv_ref[0])
    def _():
        o_ref[...] = jnp.zeros_like(o_ref)


def _moe_matmuls(l, xs, block_e, n_valid, w_gate, w_up, w_down):
    R = xs.shape[0]
    nb = R // MOE_BLOCK
    ce = 512
    live = lambda b, nv: jnp.minimum(b, nv[0] - 1)
    act = pl.pallas_call(
        _gateup_kernel,
        grid_spec=pltpu.PrefetchScalarGridSpec(
            num_scalar_prefetch=2,
            grid=(D_EXPERT // ce, nb),
            in_specs=[pl.BlockSpec((MOE_BLOCK, D_MODEL), lambda c, b, be, nv: (live(b, nv), 0)),
                      pl.BlockSpec((1, 1, D_MODEL, ce), lambda c, b, be, nv: (l, be[b], 0, c)),
                      pl.BlockSpec((1, 1, D_MODEL, ce), lambda c, b, be, nv: (l, be[b], 0, c))],
            out_specs=pl.BlockSpec((MOE_BLOCK, ce), lambda c, b, be, nv: (b, c)),
            scratch_shapes=[pltpu.VMEM((D_MODEL, ce), BF16), pltpu.VMEM((D_MODEL, ce), BF16)]),
        out_shape=jax.ShapeDtypeStruct((R, D_EXPERT), BF16),
        compiler_params=_params("arbitrary", "arbitrary"),
        name="moe_gate_up",
    )(block_e, n_valid, xs, w_gate, w_up)
    cd = 1024
    return pl.pallas_call(
        _down_kernel,
        grid_spec=pltpu.PrefetchScalarGridSpec(
            num_scalar_prefetch=2,
            grid=(D_MODEL // cd, nb),
            in_specs=[pl.BlockSpec((MOE_BLOCK, D_EXPERT), lambda c, b, be, nv: (live(b, nv), 0)),
                      pl.BlockSpec((1, 1, D_EXPERT, cd), lambda c, b, be, nv: (l, be[b], 0, c))],
            out_specs=pl.BlockSpec((MOE_BLOCK, cd), lambda c, b, be, nv: (b, c)),
            scratch_shapes=[pltpu.VMEM((D_EXPERT, cd), BF16)]),
        out_shape=jax.ShapeDtypeStruct((R, D_MODEL), F32),
        compiler_params=_params("arbitrary", "arbitrary"),
        name="moe_down",
    )(block_e, n_valid, act, w_down)


def _combine_kernel(final, pos_ref, x_ref, w0_ref, w1_ref, gt_ref, fg_ref, yb_ref, o_ref, y0_buf, y1_buf, sem):
    bb, tl, _ = x_ref.shape
    tile = pl.program_id(0) * pl.num_programs(1) + pl.program_id(1)
    base = tile * (TOP_K * ROW_TILE)

    def row_copy(src_row, buf, r):
        return pltpu.make_async_copy(yb_ref.at[pl.ds(src_row, 1), :], buf.at[pl.ds(r, 1), :], sem)

    def issue(r, carry):
        row_copy(pos_ref[base + TOP_K * r], y0_buf, r).start(priority=0)
        row_copy(pos_ref[base + TOP_K * r + 1], y1_buf, r).start(priority=1)
        return carry

    def drain(r, carry):
        row_copy(0, y0_buf, 0).wait()
        row_copy(0, y1_buf, 0).wait()
        return carry

    lax.fori_loop(0, ROW_TILE, issue, 0, unroll=8)
    lax.fori_loop(0, ROW_TILE, drain, 0, unroll=8)

    w0 = w0_ref[...]
    w1 = w1_ref[...]
    for c in range(D_MODEL // LANES):
        cols = slice(c * LANES, (c + 1) * LANES)
        ff = y0_buf[:, cols] * w0 + y1_buf[:, cols] * w1
        o_ref[:, :, cols] = x_ref[:, :, cols] + gt_ref[:, :, cols] * ff.reshape(bb, tl, LANES)
    if final:
        x2 = o_ref[...]
        ms = jnp.mean(x2 * x2, axis=-1, keepdims=True)
        o_ref[...] = x2 * lax.rsqrt(ms + RMS_EPS) * fg_ref[...]


def _combine(x1, yb, pos, w0b, w1b, mod, l, final_g, final):
    B, L, _ = x1.shape
    bb, tl = _tiles(B, L)
    nl = L // tl
    xs = pl.BlockSpec((bb, tl, D_MODEL), lambda b, i, *_: (b, i, 0))
    rows = pl.BlockSpec((bb * tl, LANES), lambda b, i, *_: (b * nl + i, 0))
    return pl.pallas_call(
        functools.partial(_combine_kernel, final),
        grid_spec=pltpu.PrefetchScalarGridSpec(
            num_scalar_prefetch=1,
            grid=(B // bb, nl),
            in_specs=[xs, rows, rows, _mod_spec(bb, l, 5),
                      pl.BlockSpec((1, D_MODEL), lambda b, i, *_: (0, 0)),
                      pl.BlockSpec(memory_space=pl.ANY)],
            out_specs=xs,
            scratch_shapes=[pltpu.VMEM((ROW_TILE, D_MODEL), F32), pltpu.VMEM((ROW_TILE, D_MODEL), F32),
                            pltpu.SemaphoreType.DMA(())]),
        out_shape=jax.ShapeDtypeStruct((B, L, D_MODEL), F32),
        compiler_params=_params("arbitrary", "arbitrary"),
        name="moe_combine",
    )(pos, x1, w0b, w1b, mod, final_g, yb)


def _vh(x, axis=-1):
    return jnp.take(x, jnp.asarray(VH_PERM), axis=axis)


def _layer_weights(l, w_in, w_pool, pool_scale, mu_shift, w0, w2, a0, a2, g2, k_k, k_a, r_k,
                   lnx_w, lnx_b, w_out, norm1_g, norm2_g):
    W = RWKV_WIDTH
    wz = w_in[l][:, POOL_WIDTH:]
    wz = jnp.concatenate([wz[:, :2 * W], _vh(wz[:, 2 * W:3 * W]), wz[:, 3 * W:],
                          jnp.zeros((D_MODEL, Z_WIDTH - SHIFT_WIDTH), F32)], axis=1).astype(BF16)
    mu = mu_shift[l]
    mu = jnp.concatenate([mu[:2 * W], _vh(mu[2 * W:3 * W]), mu[3 * W:],
                          jnp.zeros((Z_WIDTH - SHIFT_WIDTH,), F32)]).reshape(1, Z_WIDTH)
    zpad = lambda m, top, tot: jnp.concatenate(
        [jnp.zeros((top, W), F32), m, jnp.zeros((tot - top - m.shape[0], W), F32)], axis=0).astype(BF16)
    wo = w_out[l]
    wo = jnp.concatenate([wo[:POOL_WIDTH], _vh(wo[POOL_WIDTH:], axis=0)], axis=0).astype(BF16)
    r1 = lambda x: x.reshape(1, -1)
    return dict(
        wp=w_in[l][:, :POOL_WIDTH].astype(BF16), wz=wz, mu=mu,
        w_pool=w_pool[l].astype(BF16), pool_scale=r1(pool_scale[l]),
        w0=r1(w0[l]), w2=zpad(w2[l], 0, LANES), a0=r1(a0[l]), a2=zpad(a2[l], DECAY_LORA, LANES),
        g2=zpad(_vh(g2[l]), 0, Z_WIDTH - GATE_OFF),
        k_k=r1(k_k[l]), k_a=r1(k_a[l]), r_k=r1(r_k[l]),
        lnx_w=r1(_vh(lnx_w[l])), lnx_b=r1(_vh(lnx_b[l])), w_out=wo,
        norm1_g=r1(norm1_g[l]), norm2_g=r1(norm2_g[l]))


def _shift_to_z(s):
    W = RWKV_WIDTH
    out = jnp.concatenate([s[:, :2 * W], _vh(s[:, 2 * W:3 * W]), s[:, 3 * W:],
                           jnp.zeros((s.shape[0], Z_WIDTH - SHIFT_WIDTH), s.dtype)], axis=1)
    return out[:, None, :]


def _z_to_shift(zrow):
    W = RWKV_WIDTH
    v = jnp.take(zrow[:, 2 * W:3 * W], jnp.asarray(VH_INV), axis=1)
    return jnp.concatenate([zrow[:, :2 * W], v, zrow[:, 3 * W:SHIFT_WIDTH]], axis=1)


def _state_to_tile(s):
    B = s.shape[0]
    return s.transpose(0, 3, 2, 1).reshape(B, HEAD_DIM, RWKV_WIDTH // LANES, LANES)


def _tile_to_state(p):
    B = p.shape[0]
    return p.reshape(B, HEAD_DIM, HEAD_DIM, N_HEADS).transpose(0, 3, 2, 1)


def _mix_half(l, x, mod, st_wkv, st_pool, st_shift, pos0, lw, w_router_p):
    B, L, _ = x.shape
    p, z = _inproj(x, lw["norm1_g"], mod, l, lw["wp"], lw["wz"])
    st16 = jnp.concatenate([jnp.zeros((B, 1, POOL_WIDTH), F32), st_pool], axis=1)
    ypool, buf = _pool(p, st16, lw["w_pool"], lw["pool_scale"], pos0)
    aq, wb, kk, v, br, kr, bs, g = _rwkv_pre(z, _shift_to_z(st_shift), lw)
    y, s_fin = _scan((aq, wb, kk, v, br, kr), _state_to_tile(st_wkv), B, L)
    x1, h2, scores = _post(y, v, g, bs, ypool, x, lw, mod, l, w_router_p)
    return x1, h2, scores, _tile_to_state(s_fin), buf, _z_to_shift(z[:, -1])


def kernel(x_prompt, x_sample, state_wkv, state_pool, state_shift, c_prompt, c_sample, w_ada, b_ada, norm1_g, norm2_g, w_in, w_pool, pool_scale, mu_shift, w0, w2, a0, a2, g2, k_k, k_a, r_k, lnx_w, lnx_b, w_out, w_router, b_router, w_gate, w_up, w_down, final_g):
    Bp, Lp, _ = x_prompt.shape
    Bs, Ls, _ = x_sample.shape
    Tp, Ts = Bp * Lp, Bs * Ls
    mod = _ada(jnp.concatenate([c_prompt, c_sample], axis=0), w_ada, b_ada)
    mod = mod.reshape(DEPTH, Bp + Bs, 6, 1, D_MODEL).transpose(0, 2, 1, 3, 4)
    mod_p, mod_s = mod[:, :, :Bp], mod[:, :, Bp:]
    w_router_p = jnp.concatenate([w_router, jnp.zeros((D_MODEL, LANES - N_EXPERTS), F32)], axis=1)
    fg = final_g.reshape(1, D_MODEL)
    n_rows = _moe_rows(Tp + Ts) * MOE_BLOCK

    xp, xs_ = x_prompt, x_sample
    outs_p, outs_s = [], []
    for l in range(DEPTH):
        lw = _layer_weights(l, w_in, w_pool, pool_scale, mu_shift, w0, w2, a0, a2, g2, k_k, k_a, r_k,
                            lnx_w, lnx_b, w_out, norm1_g, norm2_g)
        zp = lambda *s: jnp.zeros(s, F32)
        x1p, h2p, scp, wkv_p, buf_p, shf_p = _mix_half(
            l, xp, mod_p, zp(Bp, N_HEADS, HEAD_DIM, HEAD_DIM), zp(Bp, POOL_BUF, POOL_WIDTH),
            zp(Bp, SHIFT_WIDTH), 0, lw, w_router_p)
        x1s, h2s, scs, wkv_s, buf_s, shf_s = _mix_half(
            l, xs_, mod_s, state_wkv[l], state_pool[l], state_shift[l], PAST_LEN, lw, w_router_p)
        outs_p.append((wkv_p, buf_p, shf_p))
        outs_s.append((wkv_s, buf_s, shf_s))

        scores = jnp.concatenate([scp, scs], axis=0)[:, :N_EXPERTS]
        idx, wts = _route(scores, b_router)
        dest, block_e, n_valid, pad_start, pad_cnt = _dispatch_plan(idx)
        xs_rows = _dispatch_rows(h2p, h2s, dest, pad_start, pad_cnt, n_rows)
        yb = _moe_matmuls(l, xs_rows, block_e, n_valid, w_gate, w_up, w_down)
        w0b = jnp.broadcast_to(wts[:, 0:1], (Tp + Ts, LANES))
        w1b = jnp.broadcast_to(wts[:, 1:2], (Tp + Ts, LANES))
        final = l == DEPTH - 1
        xp = _combine(x1p, yb, dest[:TOP_K * Tp], w0b[:Tp], w1b[:Tp], mod_p, l, fg, final)
        xs_ = _combine(x1s, yb, dest[TOP_K * Tp:], w0b[Tp:], w1b[Tp:], mod_s, l, fg, final)

    st = lambda outs, i: jnp.stack([o[i] for o in outs])
    return (xp, xs_, st(outs_p, 0), st(outs_p, 1), st(outs_p, 2),
            st(outs_s, 0), st(outs_s, 1), st(outs_s, 2))
```

```python
import functools

import numpy as np
import jax
import jax.numpy as jnp
from jax import lax
from jax.experimental import pallas as pl
from jax.experimental.pallas import tpu as pltpu

F32 = jnp.float32
BF16 = jnp.bfloat16

D_MODEL = 2048
DEPTH = 4
PAST_LEN = 16384
POOL_WIDTH = 1024
POOL_WINDOWS = (2, 4, 8, 16)
POOL_GROUP = 256
POOL_BUF = 15
POOL_HALO = 16
RWKV_WIDTH = 1024
HEAD_DIM = 64
N_HEADS = 16
DECAY_LORA = 64
AAA_LORA = 64
GATE_LORA = 160
SHIFT_WIDTH = 3 * RWKV_WIDTH + DECAY_LORA + AAA_LORA + GATE_LORA
Z_WIDTH = 3456
LORA_OFF = 3 * RWKV_WIDTH
GATE_OFF = LORA_OFF + 128
LNX_EPS = 64e-5
RMS_EPS = 1e-6
N_EXPERTS = 16
N_EXPERT_GROUPS = 4
EXPERTS_PER_GROUP = 4
TOP_K = 2
D_EXPERT = 1024

LANES = 128
SUBLANES = 8
ROW_TILE = 256
MOE_BLOCK = 256
VMEM_LIMIT = 56 * 1024 * 1024
HIGHEST = lax.Precision.HIGHEST


def _params(*sem, **kw):
    return pltpu.CompilerParams(dimension_semantics=sem, vmem_limit_bytes=VMEM_LIMIT, **kw)


def _sigmoid(x):
    return 1.0 / (1.0 + jnp.exp(-x))


def _tiles(B, L):
    if L >= ROW_TILE:
        assert L % ROW_TILE == 0
        return 1, ROW_TILE
    assert ROW_TILE % L == 0 and L % SUBLANES == 0 and B % (ROW_TILE // L) == 0
    return ROW_TILE // L, L


def _mod_spec(bb, l, k):
    return pl.BlockSpec((None, None, bb, 1, D_MODEL), lambda b, i, *_: (l, k, b, 0, 0))


def _ada_kernel(c_ref, w_ref, b_ref, o_ref):
    c = c_ref[...]
    s = (c * _sigmoid(c)).astype(BF16)
    o_ref[0] = jnp.dot(s, w_ref[0].astype(BF16), preferred_element_type=F32) + b_ref[0]


def _ada(c_all, w_ada, b_ada):
    bc = c_all.shape[0]
    tn = 512
    return pl.pallas_call(
        _ada_kernel,
        grid=(DEPTH, 6 * D_MODEL // tn),
        in_specs=[pl.BlockSpec((bc, D_MODEL), lambda l, n: (0, 0)),
                  pl.BlockSpec((1, D_MODEL, tn), lambda l, n: (l, 0, n)),
                  pl.BlockSpec((1, 1, tn), lambda l, n: (l, 0, n))],
        out_specs=pl.BlockSpec((1, bc, tn), lambda l, n: (l, 0, n)),
        out_shape=jax.ShapeDtypeStruct((DEPTH, bc, 6 * D_MODEL), F32),
        compiler_params=_params("arbitrary", "arbitrary"),
        name="ada_mod",
    )(c_all, w_ada, b_ada.reshape(DEPTH, 1, 6 * D_MODEL))


def _ada_norm_val(x, g, shift, scale):
    ms = jnp.mean(x * x, axis=-1, keepdims=True)
    return x * lax.rsqrt(ms + RMS_EPS) * g * (1.0 + scale) + shift


def _inproj_kernel(x_ref, g_ref, sh_ref, sc_ref, wp_ref, wz_ref, p_ref, z_ref):
    bb, tl, _ = x_ref.shape
    h = _ada_norm_val(x_ref[...], g_ref[...], sh_ref[...], sc_ref[...])
    h = h.reshape(bb * tl, D_MODEL).astype(BF16)
    p_ref[...] = jnp.dot(h, wp_ref[...], preferred_element_type=F32).reshape(bb, tl, POOL_WIDTH)
    step = 1152
    for n0 in range(0, Z_WIDTH, step):
        z_ref[:, :, n0:n0 + step] = jnp.dot(
            h, wz_ref[:, n0:n0 + step], preferred_element_type=F32).reshape(bb, tl, step)


def _inproj(x, g, mod, l, wp, wz):
    B, L, _ = x.shape
    bb, tl = _tiles(B, L)
    once = pl.Buffered(1)
    return pl.pallas_call(
        _inproj_kernel,
        grid=(B // bb, L // tl),
        in_specs=[pl.BlockSpec((bb, tl, D_MODEL), lambda b, i: (b, i, 0)),
                  pl.BlockSpec((1, D_MODEL), lambda b, i: (0, 0)),
                  _mod_spec(bb, l, 0), _mod_spec(bb, l, 1),
                  pl.BlockSpec((D_MODEL, POOL_WIDTH), lambda b, i: (0, 0), pipeline_mode=once),
                  pl.BlockSpec((D_MODEL, Z_WIDTH), lambda b, i: (0, 0), pipeline_mode=once)],
        out_specs=[pl.BlockSpec((bb, tl, POOL_WIDTH), lambda b, i: (b, i, 0)),
                   pl.BlockSpec((bb, tl, Z_WIDTH), lambda b, i: (b, i, 0))],
        out_shape=[jax.ShapeDtypeStruct((B, L, POOL_WIDTH), F32),
                   jax.ShapeDtypeStruct((B, L, Z_WIDTH), F32)],
        compiler_params=_params("arbitrary", "arbitrary"),
        name="norm_inproj",
    )(x, g, mod, mod, wp, wz)


def _pool_kernel(pos0, p_ref, halo_ref, st_ref, w_ref, scale_ref, y_ref, buf_ref, ext_ref):
    bb, tl, _ = p_ref.shape
    i = pl.program_id(1)

    @pl.when(i == 0)
    def _():
        ext_ref[:, 0:POOL_HALO, :] = st_ref[...]

    @pl.when(i > 0)
    def _():
        ext_ref[:, 0:POOL_HALO, :] = halo_ref[...]

    ext_ref[:, POOL_HALO:POOL_HALO + tl, :] = p_ref[...]
    pos = pos0 + i * tl + lax.broadcasted_iota(jnp.int32, (1, tl, POOL_GROUP), 1)
    for gi, win in enumerate(POOL_WINDOWS):
        c0 = gi * POOL_GROUP
        cur = p_ref[:, :, c0:c0 + POOL_GROUP]
        acc = cur
        for s in range(1, win):
            acc = acc + ext_ref[:, POOL_HALO - s:POOL_HALO - s + tl, c0:c0 + POOL_GROUP]
        cnt = jnp.minimum(pos + 1, win).astype(F32)
        pooled = (acc / cnt - cur).reshape(bb * tl, POOL_GROUP).astype(BF16)
        out = jnp.dot(pooled, w_ref[gi], preferred_element_type=F32) * scale_ref[:, c0:c0 + POOL_GROUP]
        y_ref[:, c0:c0 + POOL_GROUP] = out.astype(BF16)

    @pl.when(i == pl.num_programs(1) - 1)
    def _():
        buf_ref[...] = ext_ref[:, tl + 1:tl + POOL_HALO, :]


def _pool(p, st16, w_pool, scale, pos0):
    B, L, _ = p.shape
    bb, tl = _tiles(B, L)
    nl = L // tl
    if L >= POOL_HALO:
        halo, hstep = p, tl // POOL_HALO
        halo_spec = pl.BlockSpec((bb, POOL_HALO, POOL_WIDTH),
                                 lambda b, i: (b, jnp.maximum(i * hstep - 1, 0), 0))
    else:
        halo = st16
        halo_spec = pl.BlockSpec((bb, POOL_HALO, POOL_WIDTH), lambda b, i: (b, 0, 0))
    return pl.pallas_call(
        functools.partial(_pool_kernel, pos0),
        grid=(B // bb, nl),
        in_specs=[pl.BlockSpec((bb, tl, POOL_WIDTH), lambda b, i: (b, i, 0)),
                  halo_spec,
                  pl.BlockSpec((bb, POOL_HALO, POOL_WIDTH), lambda b, i: (b, 0, 0)),
                  pl.BlockSpec((len(POOL_WINDOWS), POOL_GROUP, POOL_GROUP), lambda b, i: (0, 0, 0)),
                  pl.BlockSpec((1, POOL_WIDTH), lambda b, i: (0, 0))],
        out_specs=[pl.BlockSpec((bb * tl, POOL_WIDTH), lambda b, i: (b * nl + i, 0)),
                   pl.BlockSpec((bb, POOL_BUF, POOL_WIDTH), lambda b, i: (b, 0, 0))],
        out_shape=[jax.ShapeDtypeStruct((B * L, POOL_WIDTH), BF16),
                   jax.ShapeDtypeStruct((B, POOL_BUF, POOL_WIDTH), F32)],
        scratch_shapes=[pltpu.VMEM((bb, POOL_HALO + tl, POOL_WIDTH), F32)],
        compiler_params=_params("arbitrary", "arbitrary"),
        name="pool_mixer",
    )(p, halo, st16, w_pool, scale)


def _seg_mats():
    ch = np.arange(RWKV_WIDTH) // HEAD_DIM
    ln = np.arange(LANES)
    seg = (ch[:, None] == (ln[None, :] % N_HEADS)).astype(np.float32)
    exp = ((ln[:, None] == ch[None, :])).astype(np.float32)
    return jnp.asarray(seg), jnp.asarray(exp)


def _pre_kernel(z_ref, zp_ref, st_ref, mu_ref, w0_ref, w2_ref, a0_ref, a2_ref, g2_ref,
                kk_ref, ka_ref, rk_ref, seg_ref, exp_ref,
                aq_o, wb_o, kk_o, v_o, br_o, kr_o, bs_o, g_o):
    bb, tl, _ = z_ref.shape
    i = pl.program_id(1)
    z = z_ref[...]
    first = jnp.where(i == 0, st_ref[...], zp_ref[:, SUBLANES - 1:SUBLANES, :])
    tpos = lax.broadcasted_iota(jnp.int32, (1, tl, LANES), 1)
    n = bb * tl

    def shifted(c0, c1):
        zc = z[:, :, c0:c1]
        prev = pltpu.roll(zc, 1, axis=1)
        reps = (c1 - c0) // LANES
        t0 = jnp.concatenate([tpos] * reps, axis=2) if reps > 1 else tpos
        prev = jnp.where(t0 == 0, first[:, :, c0:c1], prev)
        return (zc + (prev - zc) * mu_ref[:, c0:c1]).reshape(n, c1 - c0)

    W = RWKV_WIDTH
    r = shifted(0, W)
    k = shifted(W, 2 * W)
    v = shifted(2 * W, 3 * W)
    lora = shifted(LORA_OFF, LORA_OFF + LANES)
    gd = shifted(GATE_OFF, Z_WIDTH)

    w_lin = w0_ref[...] + jnp.dot(jnp.tanh(lora).astype(BF16), w2_ref[...], preferred_element_type=F32)
    a_lin = a0_ref[...] + jnp.dot(lora.astype(BF16), a2_ref[...], preferred_element_type=F32)
    g_o[...] = jnp.dot(_sigmoid(gd).astype(BF16), g2_ref[...], preferred_element_type=F32)
    w = -(jnp.maximum(-w_lin, 0.0) + jnp.log(1.0 + jnp.exp(-jnp.abs(w_lin)))) - 0.5
    decay = jnp.exp(-jnp.exp(w))
    a_sig = _sigmoid(a_lin)

    seg = seg_ref[...]

    def head_sum(x):
        return jnp.dot(x, seg, precision=HIGHEST, preferred_element_type=F32)

    kk = k * kk_ref[...]
    inv = 1.0 / jnp.maximum(jnp.sqrt(head_sum(kk * kk)), 1e-12)
    kk = kk * jnp.dot(inv, exp_ref[...], precision=HIGHEST, preferred_element_type=F32)
    k2 = k * (1.0 + (a_sig - 1.0) * ka_ref[...])
    b = kk * a_sig

    low_half = lax.broadcasted_iota(jnp.int32, (n, LANES), 1) < HEAD_DIM

    def store_pair(o_ref, xa, xb):
        for c in range(RWKV_WIDTH // LANES):
            cols = slice(c * LANES, (c + 1) * LANES)
            ac, bc = xa[:, cols], xb[:, cols]
            a_sw = pltpu.roll(ac, HEAD_DIM, axis=1)
            b_sw = pltpu.roll(bc, HEAD_DIM, axis=1)
            o_ref[pl.ds(2 * c, n, stride=N_HEADS), :] = jnp.where(low_half, ac, b_sw)
            o_ref[pl.ds(2 * c + 1, n, stride=N_HEADS), :] = jnp.where(low_half, a_sw, bc)

    store_pair(aq_o, -kk, decay * r)
    store_pair(wb_o, decay, b)
    store_pair(kk_o, k2, k2)
    for c in range(RWKV_WIDTH // LANES):
        v_o[pl.ds(c, n, stride=SUBLANES), :] = v[:, c * LANES:(c + 1) * LANES]
    br_o[...] = head_sum(b * r)
    kr_o[...] = head_sum(k2 * r)
    bs_o[...] = head_sum(r * k2 * rk_ref[...])


def _rwkv_pre(z, st_shift, lw):
    B, L, _ = z.shape
    bb, tl = _tiles(B, L)
    nl = L // tl
    pstep = tl // SUBLANES
    seg, exp = _seg_mats()
    row = lambda wdt: pl.BlockSpec((1, wdt), lambda b, i: (0, 0))
    full = lambda a: pl.BlockSpec(a.shape, lambda b, i: (0,) * a.ndim)
    out_w = pl.BlockSpec((bb * tl, RWKV_WIDTH), lambda b, i: (b * nl + i, 0))
    out_s = pl.BlockSpec((bb * tl, LANES), lambda b, i: (b * nl + i, 0))
    out_p = pl.BlockSpec((bb * tl * N_HEADS, LANES), lambda b, i: (b * nl + i, 0))
    out_v = pl.BlockSpec((bb * tl * SUBLANES, LANES), lambda b, i: (b * nl + i, 0))
    T = B * L
    wide = jax.ShapeDtypeStruct((T, RWKV_WIDTH), F32)
    pair = jax.ShapeDtypeStruct((T * N_HEADS, LANES), F32)
    folded = jax.ShapeDtypeStruct((T * SUBLANES, LANES), F32)
    small = jax.ShapeDtypeStruct((T, LANES), F32)
    return pl.pallas_call(
        _pre_kernel,
        grid=(B // bb, nl),
        in_specs=[pl.BlockSpec((bb, tl, Z_WIDTH), lambda b, i: (b, i, 0)),
                  pl.BlockSpec((bb, SUBLANES, Z_WIDTH), lambda b, i: (b, jnp.maximum(i * pstep - 1, 0), 0)),
                  pl.BlockSpec((bb, 1, Z_WIDTH), lambda b, i: (b, 0, 0)),
                  row(Z_WIDTH), row(RWKV_WIDTH), full(lw["w2"]), row(RWKV_WIDTH), full(lw["a2"]),
                  full(lw["g2"]), row(RWKV_WIDTH), row(RWKV_WIDTH), row(RWKV_WIDTH), full(seg), full(exp)],
        out_specs=[out_p] * 3 + [out_v] + [out_s] * 3 + [out_w],
        out_shape=[pair] * 3 + [folded] + [small] * 3 + [wide],
        compiler_params=_params("arbitrary", "arbitrary"),
        name="rwkv_pre",
    )(z, z, st_shift, lw["mu"], lw["w0"], lw["w2"], lw["a0"], lw["a2"], lw["g2"],
      lw["k_k"], lw["k_a"], lw["r_k"], seg, exp)


N_ACC = 4


def _scan_kernel(aq_ref, wb_ref, kk_ref, v_ref, br_ref, kr_ref, s0_ref, y_ref, s_ref, op_tiles, op_first):
    tc = v_ref.shape[0]
    rows = SUBLANES * N_HEADS
    n_groups = tc // SUBLANES

    @pl.when(pl.program_id(1) == 0)
    def _():
        s_ref[...] = s0_ref[...]

    def keys_to_sublanes(blk):
        return jnp.concatenate([blk] * SUBLANES, axis=0).T

    def put_pair(op_ref, slot, first, tile):
        op_ref[slot, first] = tile[0:HEAD_DIM]
        op_ref[slot, first + 1] = tile[HEAD_DIM:]

    def tree_sum(parts):
        while len(parts) > 1:
            parts = [parts[i] + parts[i + 1] for i in range(0, len(parts), 2)]
        return parts[0]

    def reduce_keys(op_ref, slot, state_of):
        sa_p = [None] * N_ACC
        yq_p = [None] * N_ACC
        for j in range(HEAD_DIM):
            sj = state_of(j)
            pa = sj * op_ref[slot, 0, pl.ds(j, 1), :]
            pq = sj * op_ref[slot, 1, pl.ds(j, 1), :]
            i = j % N_ACC
            sa_p[i] = pa if sa_p[i] is None else sa_p[i] + pa
            yq_p[i] = pq if yq_p[i] is None else yq_p[i] + pq
        return tree_sum(sa_p), tree_sum(yq_p)

    def prepare(g, op_ref):
        r0 = pl.multiple_of(jnp.minimum(g, n_groups - 1) * rows, rows)
        r_next = pl.multiple_of(jnp.minimum(g + 1, n_groups - 1) * rows, rows)
        aq = aq_ref[pl.ds(r0, rows), :]
        wb = wb_ref[pl.ds(r0, rows), :]
        kx = kk_ref[pl.ds(r0, rows), :]
        aq_next = aq_ref[pl.ds(r_next, N_HEADS), :]
        for t in range(SUBLANES):
            nxt = aq[(t + 1) * N_HEADS:(t + 2) * N_HEADS] if t + 1 < SUBLANES else aq_next
            put_pair(op_ref, t, 0, keys_to_sublanes(nxt))
            put_pair(op_ref, t, 2, keys_to_sublanes(wb[t * N_HEADS:(t + 1) * N_HEADS]))
            op_ref[t, 4] = keys_to_sublanes(kx[t * N_HEADS:(t + 1) * N_HEADS])[0:HEAD_DIM]

    def run_group(g, carry, op_ref):
        sa, yq = carry
        for t in range(SUBLANES):
            step = g * SUBLANES + t
            v = v_ref[step]
            y_ref[step] = yq + sa * br_ref[pl.ds(step, 1), :] + v * kr_ref[pl.ds(step, 1), :]

            def updated(j, t=t, sa=sa, v=v):
                sj = (s_ref[0, j] * op_ref[t, 2, pl.ds(j, 1), :] + sa * op_ref[t, 3, pl.ds(j, 1), :]
                      + v * op_ref[t, 4, pl.ds(j, 1), :])
                s_ref[0, j] = sj
                return sj

            sa, yq = reduce_keys(op_ref, t, updated)
        return sa, yq

    put_pair(op_first, 0, 0, keys_to_sublanes(aq_ref[0:N_HEADS, :]))
    first = reduce_keys(op_first, 0, lambda j: s_ref[0, j])

    def one_group(g, carry):
        prepare(g, op_tiles)
        return run_group(g, carry, op_tiles)

    lax.fori_loop(0, n_groups, one_group, first)


def _scan(ops, s0, B, L):
    aq, wb, kk, v, br, kr = ops
    tc = min(L, 128)
    nc = L // tc
    T = B * L
    nblk = RWKV_WIDTH // LANES
    op_spec = pl.BlockSpec((tc * N_HEADS, LANES), lambda bi, c: (bi * nc + c, 0))
    v_spec = pl.BlockSpec((tc, nblk, LANES), lambda bi, c: (bi * nc + c, 0, 0))
    s_spec = pl.BlockSpec((tc, LANES), lambda bi, c: (bi * nc + c, 0))
    st_spec = pl.BlockSpec((1, HEAD_DIM, nblk, LANES), lambda bi, c: (bi, 0, 0, 0))
    y, s = pl.pallas_call(
        _scan_kernel,
        grid=(B, nc),
        in_specs=[op_spec] * 3 + [v_spec, s_spec, s_spec, st_spec],
        out_specs=[v_spec, st_spec],
        out_shape=[jax.ShapeDtypeStruct((T, nblk, LANES), F32),
                   jax.ShapeDtypeStruct((B, HEAD_DIM, nblk, LANES), F32)],
        scratch_shapes=[pltpu.VMEM((SUBLANES, 5, HEAD_DIM, LANES), F32),
                        pltpu.VMEM((1, 2, HEAD_DIM, LANES), F32)],
        compiler_params=_params("arbitrary", "arbitrary"),
        name="rwkv_scan",
    )(aq, wb, kk, v.reshape(T, nblk, LANES), br, kr, s0)
    return y.reshape(T * nblk, LANES), s


def _post_kernel(y_ref, v_ref, g_ref, bs_ref, yp_ref, x_ref, lw_ref, lb_ref, wo_ref,
                 gta_ref, n2_ref, sh_ref, sc_ref, wr_ref, x1_ref, h2_ref, s_ref):
    bb, tl, _ = x_ref.shape
    nblk = RWKV_WIDTH // LANES
    n = bb * tl
    cols = [slice(c * LANES, (c + 1) * LANES) for c in range(nblk)]
    folded = lambda ref, c: ref[pl.ds(c, n, stride=nblk), :]
    y = [folded(y_ref, c) for c in range(nblk)]

    def head_total(parts):
        s = parts[0]
        for part in parts[1:]:
            s = s + part
        for sh in (N_HEADS, 2 * N_HEADS, 4 * N_HEADS):
            s = s + pltpu.roll(s, sh, axis=1)
        return s

    mean = head_total(y) * (1.0 / HEAD_DIM)
    dev = [yc - mean for yc in y]
    var = head_total([d * d for d in dev]) * (1.0 / HEAD_DIM)
    rstd = lax.rsqrt(var + LNX_EPS)
    bs = bs_ref[...]
    outs = []
    for ci, c in enumerate(cols):
        yn = dev[ci] * rstd * lw_ref[:, c] + lb_ref[:, c]
        outs.append(((yn + bs * folded(v_ref, ci)) * g_ref[:, c]).astype(BF16))
    yr = jnp.concatenate(outs, axis=1)
    mix = jnp.dot(yp_ref[...], wo_ref[0:POOL_WIDTH, :], preferred_element_type=F32)
    mix = mix + jnp.dot(yr, wo_ref[POOL_WIDTH:, :], preferred_element_type=F32)
    x1 = x_ref[...] + gta_ref[...] * mix.reshape(bb, tl, D_MODEL)
    x1_ref[...] = x1
    h2 = _ada_norm_val(x1, n2_ref[...], sh_ref[...], sc_ref[...]).reshape(bb * tl, D_MODEL)
    h2_ref[...] = h2
    s_ref[...] = _sigmoid(jnp.dot(h2, wr_ref[...], precision=HIGHEST, preferred_element_type=F32))


def _post(y, v, g, bs, ypool, x, lw, mod, l, w_router_p):
    B, L, _ = x.shape
    bb, tl = _tiles(B, L)
    nl = L // tl
    T = B * L
    rows = lambda wdt: pl.BlockSpec((bb * tl, wdt), lambda b, i: (b * nl + i, 0))
    fold_rows = pl.BlockSpec((bb * tl * RWKV_WIDTH // LANES, LANES), lambda b, i: (b * nl + i, 0))
    row = lambda wdt: pl.BlockSpec((1, wdt), lambda b, i: (0, 0))
    xs = pl.BlockSpec((bb, tl, D_MODEL), lambda b, i: (b, i, 0))
    return pl.pallas_call(
        _post_kernel,
        grid=(B // bb, nl),
        in_specs=[fold_rows, fold_rows, rows(RWKV_WIDTH), rows(LANES), rows(POOL_WIDTH), xs,
                  row(RWKV_WIDTH), row(RWKV_WIDTH),
                  pl.BlockSpec((D_MODEL, D_MODEL), lambda b, i: (0, 0)),
                  _mod_spec(bb, l, 2), row(D_MODEL), _mod_spec(bb, l, 3), _mod_spec(bb, l, 4),
                  pl.BlockSpec((D_MODEL, LANES), lambda b, i: (0, 0))],
        out_specs=[xs, rows(D_MODEL), rows(LANES)],
        out_shape=[jax.ShapeDtypeStruct((B, L, D_MODEL), F32),
                   jax.ShapeDtypeStruct((T, D_MODEL), F32),
                   jax.ShapeDtypeStruct((T, LANES), F32)],
        compiler_params=_params("arbitrary", "arbitrary"),
        name="post_outproj_norm2",
    )(y, v, g, bs, ypool, x, lw["lnx_w"], lw["lnx_b"], lw["w_out"], mod, lw["norm2_g"], mod, mod, w_router_p)


def _route(scores, b_router):
    T = scores.shape[0]
    sel = scores + b_router.astype(F32)
    neg = jnp.float32(-jnp.inf)

    def top2(x):
        lane = lax.broadcasted_iota(jnp.int32, x.shape, x.ndim - 1)
        i0 = jnp.argmax(x, -1)
        m0 = jnp.max(x, -1)
        rest = jnp.where(lane == i0[..., None], neg, x)
        return m0, jnp.max(rest, -1), i0, jnp.argmax(rest, -1)

    g0, g1, _, _ = top2(sel.reshape(T, N_EXPERT_GROUPS, EXPERTS_PER_GROUP))
    gbest = jnp.argmax(g0 + g1, -1)
    in_group = (jnp.arange(N_EXPERTS) // EXPERTS_PER_GROUP)[None, :] == gbest[:, None]
    _, _, i0, i1 = top2(jnp.where(in_group, sel, neg))
    idx = jnp.stack([i0, i1], axis=1).astype(jnp.int32)
    wts = jnp.take_along_axis(scores, idx, 1)
    wts = wts / jnp.sum(wts, -1, keepdims=True)
    return idx, wts


def _moe_rows(T):
    return -(-(T * TOP_K + N_EXPERTS * (MOE_BLOCK - 1)) // MOE_BLOCK)


def _dispatch_plan(idx):
    T = idx.shape[0]
    A = T * TOP_K
    nb = _moe_rows(T)
    e_flat = idx.reshape(A)
    experts = jnp.arange(N_EXPERTS, dtype=jnp.int32)
    onehot = (e_flat[:, None] == experts[None, :]).astype(jnp.int32)
    csum = jnp.cumsum(onehot, axis=0)
    rank = jnp.sum(csum * onehot, axis=1) - 1
    counts = csum[-1]
    padded = (counts + MOE_BLOCK - 1) // MOE_BLOCK * MOE_BLOCK
    pend = jnp.cumsum(padded)
    pstart = pend - padded
    dest = (jnp.sum(onehot * pstart[None, :], axis=1) + rank).astype(jnp.int32)
    blk0 = jnp.arange(nb, dtype=jnp.int32) * MOE_BLOCK
    block_e = jnp.sum((pend[None, :] <= blk0[:, None]).astype(jnp.int32), axis=1)
    block_e = jnp.minimum(block_e, N_EXPERTS - 1).astype(jnp.int32)
    n_valid = (pend[-1] // MOE_BLOCK).astype(jnp.int32).reshape(1)
    pad_start = jnp.concatenate([pstart + counts, pend[-1:]]).astype(jnp.int32)
    pad_cnt = jnp.concatenate([padded - counts, nb * MOE_BLOCK - pend[-1:]]).astype(jnp.int32)
    return dest, block_e, n_valid, pad_start, pad_cnt


def _dispatch_kernel(tiles_a, dest_ref, pad_start_ref, pad_cnt_ref, ha_ref, hb_ref, xs_ref, sem):
    i = pl.program_id(0)
    base = i * (TOP_K * ROW_TILE)

    def row_copy(h_ref, src_row, dst_row):
        return pltpu.make_async_copy(h_ref.at[pl.ds(src_row, 1), :], xs_ref.at[pl.ds(dst_row, 1), :], sem)

    def scatter_tile(h_ref):
        def issue(r, carry):
            for k in range(TOP_K):
                row_copy(h_ref, r, dest_ref[base + TOP_K * r + k]).start(priority=k)
            return carry

        def drain(r, carry):
            for k in range(TOP_K):
                row_copy(h_ref, 0, 0).wait()
            return carry

        lax.fori_loop(0, ROW_TILE, issue, 0, unroll=8)
        lax.fori_loop(0, ROW_TILE, drain, 0, unroll=8)

    @pl.when(i < tiles_a)
    def _():
        scatter_tile(ha_ref)

    @pl.when(i >= tiles_a)
    def _():
        scatter_tile(hb_ref)

    def fill(e, wait):
        start, cnt = pad_start_ref[e], pad_cnt_ref[e]
        go = (lambda cp: cp.wait()) if wait else (lambda cp: cp.start())
        head = jnp.minimum((-start) & (SUBLANES - 1), cnt)

        def chunk(size, at):
            at = pl.multiple_of(at, SUBLANES)
            return pltpu.make_async_copy(ha_ref.at[pl.ds(0, size), :], xs_ref.at[pl.ds(at, size), :], sem)

        def one(j, carry):
            go(row_copy(ha_ref, 0, start + j))
            return carry

        lax.fori_loop(0, head, one, 0)
        pos = start + head
        rem = cnt - head

        def full(j, carry):
            go(chunk(ROW_TILE, pos + j * ROW_TILE))
            return carry

        n_full = rem // ROW_TILE
        lax.fori_loop(0, n_full, full, 0)
        pos = pos + n_full * ROW_TILE
        size = ROW_TILE // 2
        while size >= SUBLANES:
            take = (rem & size) != 0

            @pl.when(take)
            def _(size=size, pos=pos):
                go(chunk(size, pos))

            pos = pos + jnp.where(take, size, 0)
            size //= 2

    @pl.when(i == 0)
    def _():
        for e in range(N_EXPERTS + 1):
            fill(e, False)
        for e in range(N_EXPERTS + 1):
            fill(e, True)


def _dispatch_rows(ha, hb, dest, pad_start, pad_cnt, n_rows):
    tiles_a, tiles_b = ha.shape[0] // ROW_TILE, hb.shape[0] // ROW_TILE
    return pl.pallas_call(
        functools.partial(_dispatch_kernel, tiles_a),
        grid_spec=pltpu.PrefetchScalarGridSpec(
            num_scalar_prefetch=3,
            grid=(tiles_a + tiles_b,),
            in_specs=[pl.BlockSpec((ROW_TILE, D_MODEL), lambda i, *_: (jnp.minimum(i, tiles_a - 1), 0)),
                      pl.BlockSpec((ROW_TILE, D_MODEL), lambda i, *_: (jnp.maximum(i - tiles_a, 0), 0))],
            out_specs=pl.BlockSpec(memory_space=pl.ANY),
            scratch_shapes=[pltpu.SemaphoreType.DMA(())]),
        out_shape=jax.ShapeDtypeStruct((n_rows, D_MODEL), F32),
        compiler_params=_params("arbitrary"),
        name="moe_dispatch",
    )(dest, pad_start, pad_cnt, ha, hb)


def _expert_changed(be_ref, b):
    return jnp.logical_or(b == 0, be_ref[b] != be_ref[jnp.maximum(b - 1, 0)])


def _gateup_kernel(be_ref, nv_ref, x_ref, wg_ref, wu_ref, o_ref, wgc, wuc):
    b = pl.program_id(1)

    @pl.when(_expert_changed(be_ref, b))
    def _():
        wgc[...] = wg_ref[0, 0].astype(BF16)
        wuc[...] = wu_ref[0, 0].astype(BF16)

    @pl.when(b < nv_ref[0])
    def _():
        x = x_ref[...].astype(BF16)
        gate = jnp.dot(x, wgc[...], preferred_element_type=F32)
        up = jnp.dot(x, wuc[...], preferred_element_type=F32)
        o_ref[...] = (gate * _sigmoid(gate) * up).astype(BF16)

    @pl.when(b >= nv_ref[0])
    def _():
        o_ref[...] = jnp.zeros_like(o_ref)


def _down_kernel(be_ref, nv_ref, a_ref, wd_ref, o_ref, wdc):
    b = pl.program_id(1)

    @pl.when(_expert_changed(be_ref, b))
    def _():
        wdc[...] = wd_ref[0, 0].astype(BF16)

    @pl.when(b < nv_ref[0])
    def _():
        o_ref[...] = jnp.dot(a_ref[...], wdc[...], preferred_element_type=F32)

    @pl.when(b >= nv_ref[0])
    def _():
        o_ref[...] = jnp.zeros_like(o_ref)


def _moe_matmuls(l, xs, block_e, n_valid, w_gate, w_up, w_down):
    R = xs.shape[0]
    nb = R // MOE_BLOCK
    ce = D_EXPERT
    live = lambda b, nv: jnp.minimum(b, nv[0] - 1)
    act = pl.pallas_call(
        _gateup_kernel,
        grid_spec=pltpu.PrefetchScalarGridSpec(
            num_scalar_prefetch=2,
            grid=(D_EXPERT // ce, nb),
            in_specs=[pl.BlockSpec((MOE_BLOCK, D_MODEL), lambda c, b, be, nv: (live(b, nv), 0)),
                      pl.BlockSpec((1, 1, D_MODEL, ce), lambda c, b, be, nv: (l, be[b], 0, c)),
                      pl.BlockSpec((1, 1, D_MODEL, ce), lambda c, b, be, nv: (l, be[b], 0, c))],
            out_specs=pl.BlockSpec((MOE_BLOCK, ce), lambda c, b, be, nv: (b, c)),
            scratch_shapes=[pltpu.VMEM((D_MODEL, ce), BF16), pltpu.VMEM((D_MODEL, ce), BF16)]),
        out_shape=jax.ShapeDtypeStruct((R, D_EXPERT), BF16),
        compiler_params=_params("arbitrary", "arbitrary"),
        name="moe_gate_up",
    )(block_e, n_valid, xs, w_gate, w_up)
    cd = D_MODEL
    return pl.pallas_call(
        _down_kernel,
        grid_spec=pltpu.PrefetchScalarGridSpec(
            num_scalar_prefetch=2,
            grid=(D_MODEL // cd, nb),
            in_specs=[pl.BlockSpec((MOE_BLOCK, D_EXPERT), lambda c, b, be, nv: (live(b, nv), 0)),
                      pl.BlockSpec((1, 1, D_EXPERT, cd), lambda c, b, be, nv: (l, be[b], 0, c))],
            out_specs=pl.BlockSpec((MOE_BLOCK, cd), lambda c, b, be, nv: (b, c)),
            scratch_shapes=[pltpu.VMEM((D_EXPERT, cd), BF16)]),
        out_shape=jax.ShapeDtypeStruct((R, D_MODEL), F32),
        compiler_params=_params("arbitrary", "arbitrary"),
        name="moe_down",
    )(block_e, n_valid, act, w_down)


def _combine_kernel(final, pos_ref, x_ref, w0_ref, w1_ref, gt_ref, fg_ref, yb_ref, o_ref, y0_buf, y1_buf, sem):
    bb, tl, _ = x_ref.shape
    tile = pl.program_id(0) * pl.num_programs(1) + pl.program_id(1)
    base = tile * (TOP_K * ROW_TILE)

    def row_copy(src_row, buf, r):
        return pltpu.make_async_copy(yb_ref.at[pl.ds(src_row, 1), :], buf.at[pl.ds(r, 1), :], sem)

    def issue(r, carry):
        row_copy(pos_ref[base + TOP_K * r], y0_buf, r).start(priority=0)
        row_copy(pos_ref[base + TOP_K * r + 1], y1_buf, r).start(priority=1)
        return carry

    def drain(r, carry):
        row_copy(0, y0_buf, 0).wait()
        row_copy(0, y1_buf, 0).wait()
        return carry

    lax.fori_loop(0, ROW_TILE, issue, 0, unroll=8)
    lax.fori_loop(0, ROW_TILE, drain, 0, unroll=8)

    w0 = w0_ref[...]
    w1 = w1_ref[...]
    for c in range(D_MODEL // LANES):
        cols = slice(c * LANES, (c + 1) * LANES)
        ff = y0_buf[:, cols] * w0 + y1_buf[:, cols] * w1
        o_ref[:, :, cols] = x_ref[:, :, cols] + gt_ref[:, :, cols] * ff.reshape(bb, tl, LANES)
    if final:
        x2 = o_ref[...]
        ms = jnp.mean(x2 * x2, axis=-1, keepdims=True)
        o_ref[...] = x2 * lax.rsqrt(ms + RMS_EPS) * fg_ref[...]


def _combine(x1, yb, pos, w0b, w1b, mod, l, final_g, final):
    B, L, _ = x1.shape
    bb, tl = _tiles(B, L)
    nl = L // tl
    xs = pl.BlockSpec((bb, tl, D_MODEL), lambda b, i, *_: (b, i, 0))
    rows = pl.BlockSpec((bb * tl, LANES), lambda b, i, *_: (b * nl + i, 0))
    return pl.pallas_call(
        functools.partial(_combine_kernel, final),
        grid_spec=pltpu.PrefetchScalarGridSpec(
            num_scalar_prefetch=1,
            grid=(B // bb, nl),
            in_specs=[xs, rows, rows, _mod_spec(bb, l, 5),
                      pl.BlockSpec((1, D_MODEL), lambda b, i, *_: (0, 0)),
                      pl.BlockSpec(memory_space=pl.ANY)],
            out_specs=xs,
            scratch_shapes=[pltpu.VMEM((ROW_TILE, D_MODEL), F32), pltpu.VMEM((ROW_TILE, D_MODEL), F32),
                            pltpu.SemaphoreType.DMA(())]),
        out_shape=jax.ShapeDtypeStruct((B, L, D_MODEL), F32),
        compiler_params=_params("arbitrary", "arbitrary"),
        name="moe_combine",
    )(pos, x1, w0b, w1b, mod, final_g, yb)


def _vh(x, axis=-1):
    axis = axis % x.ndim
    split = x.reshape(x.shape[:axis] + (N_HEADS, HEAD_DIM) + x.shape[axis + 1:])
    return jnp.swapaxes(split, axis, axis + 1).reshape(x.shape)


def _vh_inverse(x, axis=-1):
    axis = axis % x.ndim
    split = x.reshape(x.shape[:axis] + (HEAD_DIM, N_HEADS) + x.shape[axis + 1:])
    return jnp.swapaxes(split, axis, axis + 1).reshape(x.shape)


def _layer_weights(l, w_in, w_pool, pool_scale, mu_shift, w0, w2, a0, a2, g2, k_k, k_a, r_k,
                   lnx_w, lnx_b, w_out, norm1_g, norm2_g):
    W = RWKV_WIDTH
    wz = w_in[l][:, POOL_WIDTH:]
    wz = jnp.concatenate([wz[:, :2 * W], _vh(wz[:, 2 * W:3 * W]), wz[:, 3 * W:],
                          jnp.zeros((D_MODEL, Z_WIDTH - SHIFT_WIDTH), F32)], axis=1).astype(BF16)
    mu = mu_shift[l]
    mu = jnp.concatenate([mu[:2 * W], _vh(mu[2 * W:3 * W]), mu[3 * W:],
                          jnp.zeros((Z_WIDTH - SHIFT_WIDTH,), F32)]).reshape(1, Z_WIDTH)
    zpad = lambda m, top, tot: jnp.concatenate(
        [jnp.zeros((top, W), F32), m, jnp.zeros((tot - top - m.shape[0], W), F32)], axis=0).astype(BF16)
    wo = w_out[l]
    wo = jnp.concatenate([wo[:POOL_WIDTH], _vh(wo[POOL_WIDTH:], axis=0)], axis=0).astype(BF16)
    r1 = lambda x: x.reshape(1, -1)
    return dict(
        wp=w_in[l][:, :POOL_WIDTH].astype(BF16), wz=wz, mu=mu,
        w_pool=w_pool[l].astype(BF16), pool_scale=r1(pool_scale[l]),
        w0=r1(w0[l]), w2=zpad(w2[l], 0, LANES), a0=r1(a0[l]), a2=zpad(a2[l], DECAY_LORA, LANES),
        g2=zpad(_vh(g2[l]), 0, Z_WIDTH - GATE_OFF),
        k_k=r1(k_k[l]), k_a=r1(k_a[l]), r_k=r1(r_k[l]),
        lnx_w=r1(_vh(lnx_w[l])), lnx_b=r1(_vh(lnx_b[l])), w_out=wo,
        norm1_g=r1(norm1_g[l]), norm2_g=r1(norm2_g[l]))


def _shift_to_z(s):
    W = RWKV_WIDTH
    out = jnp.concatenate([s[:, :2 * W], _vh(s[:, 2 * W:3 * W]), s[:, 3 * W:],
                           jnp.zeros((s.shape[0], Z_WIDTH - SHIFT_WIDTH), s.dtype)], axis=1)
    return out[:, None, :]


def _z_to_shift(zrow):
    W = RWKV_WIDTH
    v = _vh_inverse(zrow[:, 2 * W:3 * W])
    return jnp.concatenate([zrow[:, :2 * W], v, zrow[:, 3 * W:SHIFT_WIDTH]], axis=1)


def _state_to_tile(s):
    B = s.shape[0]
    return s.transpose(0, 3, 2, 1).reshape(B, HEAD_DIM, RWKV_WIDTH // LANES, LANES)


def _tile_to_state(p):
    B = p.shape[0]
    return p.reshape(B, HEAD_DIM, HEAD_DIM, N_HEADS).transpose(0, 3, 2, 1)


def _mix_half(l, x, mod, st_wkv, st_pool, st_shift, pos0, lw, w_router_p):
    B, L, _ = x.shape
    p, z = _inproj(x, lw["norm1_g"], mod, l, lw["wp"], lw["wz"])
    st16 = jnp.concatenate([jnp.zeros((B, 1, POOL_WIDTH), F32), st_pool], axis=1)
    ypool, buf = _pool(p, st16, lw["w_pool"], lw["pool_scale"], pos0)
    aq, wb, kk, v, br, kr, bs, g = _rwkv_pre(z, _shift_to_z(st_shift), lw)
    y, s_fin = _scan((aq, wb, kk, v, br, kr), _state_to_tile(st_wkv), B, L)
    x1, h2, scores = _post(y, v, g, bs, ypool, x, lw, mod, l, w_router_p)
    return x1, h2, scores, _tile_to_state(s_fin), buf, _z_to_shift(z[:, -1])


def kernel(x_prompt, x_sample, state_wkv, state_pool, state_shift, c_prompt, c_sample, w_ada, b_ada, norm1_g, norm2_g, w_in, w_pool, pool_scale, mu_shift, w0, w2, a0, a2, g2, k_k, k_a, r_k, lnx_w, lnx_b, w_out, w_router, b_router, w_gate, w_up, w_down, final_g):
    Bp, Lp, _ = x_prompt.shape
    Bs, Ls, _ = x_sample.shape
    Tp, Ts = Bp * Lp, Bs * Ls
    mod = _ada(jnp.concatenate([c_prompt, c_sample], axis=0), w_ada, b_ada)
    mod = mod.reshape(DEPTH, Bp + Bs, 6, 1, D_MODEL).transpose(0, 2, 1, 3, 4)
    mod_p, mod_s = mod[:, :, :Bp], mod[:, :, Bp:]
    w_router_p = jnp.concatenate([w_router, jnp.zeros((D_MODEL, LANES - N_EXPERTS), F32)], axis=1)
    fg = final_g.reshape(1, D_MODEL)
    n_rows = _moe_rows(Tp + Ts) * MOE_BLOCK

    xp, xs_ = x_prompt, x_sample
    outs_p, outs_s = [], []
    for l in range(DEPTH):
        lw = _layer_weights(l, w_in, w_pool, pool_scale, mu_shift, w0, w2, a0, a2, g2, k_k, k_a, r_k,
                            lnx_w, lnx_b, w_out, norm1_g, norm2_g)
        zp = lambda *s: jnp.zeros(s, F32)
        x1p, h2p, scp, wkv_p, buf_p, shf_p = _mix_half(
            l, xp, mod_p, zp(Bp, N_HEADS, HEAD_DIM, HEAD_DIM), zp(Bp, POOL_BUF, POOL_WIDTH),
            zp(Bp, SHIFT_WIDTH), 0, lw, w_router_p)
        x1s, h2s, scs, wkv_s, buf_s, shf_s = _mix_half(
            l, xs_, mod_s, state_wkv[l], state_pool[l], state_shift[l], PAST_LEN, lw, w_router_p)
        outs_p.append((wkv_p, buf_p, shf_p))
        outs_s.append((wkv_s, buf_s, shf_s))

        scores = jnp.concatenate([scp, scs], axis=0)[:, :N_EXPERTS]
        idx, wts = _route(scores, b_router)
        dest, block_e, n_valid, pad_start, pad_cnt = _dispatch_plan(idx)
        xs_rows = _dispatch_rows(h2p, h2s, dest, pad_start, pad_cnt, n_rows)
        yb = _moe_matmuls(l, xs_rows, block_e, n_valid, w_gate, w_up, w_down)
        w0b = jnp.broadcast_to(wts[:, 0:1], (Tp + Ts, LANES))
        w1b = jnp.broadcast_to(wts[:, 1:2], (Tp + Ts, LANES))
        final = l == DEPTH - 1
        xp = _combine(x1p, yb, dest[:TOP_K * Tp], w0b[:Tp], w1b[:Tp], mod_p, l, fg, final)
        xs_ = _combine(x1s, yb, dest[TOP_K * Tp:], w0b[Tp:], w1b[Tp:], mod_s, l, fg, final)

    st = lambda outs, i: jnp.stack([o[i] for o in outs])
    return (xp, xs_, st(outs_p, 0), st(outs_p, 1), st(outs_p, 2),
            st(outs_s, 0), st(outs_s, 1), st(outs_s, 2))
```

```python
import functools

import numpy as np
import jax
import jax.numpy as jnp
from jax import lax
from jax.experimental import pallas as pl
from jax.experimental.pallas import tpu as pltpu

F32 = jnp.float32
BF16 = jnp.bfloat16

D_MODEL = 2048
DEPTH = 4
PAST_LEN = 16384
POOL_WIDTH = 1024
POOL_WINDOWS = (2, 4, 8, 16)
POOL_GROUP = 256
POOL_BUF = 15
POOL_HALO = 16
RWKV_WIDTH = 1024
HEAD_DIM = 64
N_HEADS = 16
DECAY_LORA = 64
AAA_LORA = 64
GATE_LORA = 160
SHIFT_WIDTH = 3 * RWKV_WIDTH + DECAY_LORA + AAA_LORA + GATE_LORA
Z_WIDTH = 3456
LORA_OFF = 3 * RWKV_WIDTH
GATE_OFF = LORA_OFF + 128
LNX_EPS = 64e-5
RMS_EPS = 1e-6
N_EXPERTS = 16
N_EXPERT_GROUPS = 4
EXPERTS_PER_GROUP = 4
TOP_K = 2
D_EXPERT = 1024

LANES = 128
SUBLANES = 8
ROW_TILE = 256
MOE_BLOCK = 256
VMEM_LIMIT = 56 * 1024 * 1024
HIGHEST = lax.Precision.HIGHEST


def _params(*sem, **kw):
    return pltpu.CompilerParams(dimension_semantics=sem, vmem_limit_bytes=VMEM_LIMIT, **kw)


def _sigmoid(x):
    return 1.0 / (1.0 + jnp.exp(-x))


def _tiles(B, L):
    if L >= ROW_TILE:
        assert L % ROW_TILE == 0
        return 1, ROW_TILE
    assert ROW_TILE % L == 0 and L % SUBLANES == 0 and B % (ROW_TILE // L) == 0
    return ROW_TILE // L, L


def _mod_spec(bb, l, k):
    return pl.BlockSpec((None, None, bb, 1, D_MODEL), lambda b, i, *_: (l, k, b, 0, 0))


def _ada_kernel(c_ref, w_ref, b_ref, o_ref):
    c = c_ref[...]
    s = (c * _sigmoid(c)).astype(BF16)
    o_ref[0] = jnp.dot(s, w_ref[0].astype(BF16), preferred_element_type=F32) + b_ref[0]


def _ada(c_all, w_ada, b_ada):
    bc = c_all.shape[0]
    tn = 512
    return pl.pallas_call(
        _ada_kernel,
        grid=(DEPTH, 6 * D_MODEL // tn),
        in_specs=[pl.BlockSpec((bc, D_MODEL), lambda l, n: (0, 0)),
                  pl.BlockSpec((1, D_MODEL, tn), lambda l, n: (l, 0, n)),
                  pl.BlockSpec((1, 1, tn), lambda l, n: (l, 0, n))],
        out_specs=pl.BlockSpec((1, bc, tn), lambda l, n: (l, 0, n)),
        out_shape=jax.ShapeDtypeStruct((DEPTH, bc, 6 * D_MODEL), F32),
        compiler_params=_params("arbitrary", "arbitrary"),
        name="ada_mod",
    )(c_all, w_ada, b_ada.reshape(DEPTH, 1, 6 * D_MODEL))


def _ada_norm_val(x, g, shift, scale):
    ms = jnp.mean(x * x, axis=-1, keepdims=True)
    return x * lax.rsqrt(ms + RMS_EPS) * g * (1.0 + scale) + shift


def _inproj_kernel(x_ref, g_ref, sh_ref, sc_ref, wp_ref, wz_ref, p_ref, z_ref):
    bb, tl, _ = x_ref.shape
    h = _ada_norm_val(x_ref[...], g_ref[...], sh_ref[...], sc_ref[...])
    h = h.reshape(bb * tl, D_MODEL).astype(BF16)
    p_ref[...] = jnp.dot(h, wp_ref[...], preferred_element_type=F32).reshape(bb, tl, POOL_WIDTH)
    step = 1152
    for n0 in range(0, Z_WIDTH, step):
        z_ref[:, :, n0:n0 + step] = jnp.dot(
            h, wz_ref[:, n0:n0 + step], preferred_element_type=F32).reshape(bb, tl, step)


def _inproj(x, g, mod, l, wp, wz):
    B, L, _ = x.shape
    bb, tl = _tiles(B, L)
    once = pl.Buffered(1)
    return pl.pallas_call(
        _inproj_kernel,
        grid=(B // bb, L // tl),
        in_specs=[pl.BlockSpec((bb, tl, D_MODEL), lambda b, i: (b, i, 0)),
                  pl.BlockSpec((1, D_MODEL), lambda b, i: (0, 0)),
                  _mod_spec(bb, l, 0), _mod_spec(bb, l, 1),
                  pl.BlockSpec((D_MODEL, POOL_WIDTH), lambda b, i: (0, 0), pipeline_mode=once),
                  pl.BlockSpec((D_MODEL, Z_WIDTH), lambda b, i: (0, 0), pipeline_mode=once)],
        out_specs=[pl.BlockSpec((bb, tl, POOL_WIDTH), lambda b, i: (b, i, 0)),
                   pl.BlockSpec((bb, tl, Z_WIDTH), lambda b, i: (b, i, 0))],
        out_shape=[jax.ShapeDtypeStruct((B, L, POOL_WIDTH), F32),
                   jax.ShapeDtypeStruct((B, L, Z_WIDTH), F32)],
        compiler_params=_params("arbitrary", "arbitrary"),
        name="norm_inproj",
    )(x, g, mod, mod, wp, wz)


def _pool_kernel(pos0, p_ref, halo_ref, st_ref, w_ref, scale_ref, y_ref, buf_ref, ext_ref):
    bb, tl, _ = p_ref.shape
    i = pl.program_id(1)

    @pl.when(i == 0)
    def _():
        ext_ref[:, 0:POOL_HALO, :] = st_ref[...]

    @pl.when(i > 0)
    def _():
        ext_ref[:, 0:POOL_HALO, :] = halo_ref[...]

    ext_ref[:, POOL_HALO:POOL_HALO + tl, :] = p_ref[...]
    pos = pos0 + i * tl + lax.broadcasted_iota(jnp.int32, (1, tl, POOL_GROUP), 1)
    for gi, win in enumerate(POOL_WINDOWS):
        c0 = gi * POOL_GROUP
        cur = p_ref[:, :, c0:c0 + POOL_GROUP]
        acc = cur
        for s in range(1, win):
            acc = acc + ext_ref[:, POOL_HALO - s:POOL_HALO - s + tl, c0:c0 + POOL_GROUP]
        cnt = jnp.minimum(pos + 1, win).astype(F32)
        pooled = (acc / cnt - cur).reshape(bb * tl, POOL_GROUP).astype(BF16)
        out = jnp.dot(pooled, w_ref[gi], preferred_element_type=F32) * scale_ref[:, c0:c0 + POOL_GROUP]
        y_ref[:, c0:c0 + POOL_GROUP] = out.astype(BF16)

    @pl.when(i == pl.num_programs(1) - 1)
    def _():
        buf_ref[...] = ext_ref[:, tl + 1:tl + POOL_HALO, :]


def _pool(p, st16, w_pool, scale, pos0):
    B, L, _ = p.shape
    bb, tl = _tiles(B, L)
    nl = L // tl
    if L >= POOL_HALO:
        halo, hstep = p, tl // POOL_HALO
        halo_spec = pl.BlockSpec((bb, POOL_HALO, POOL_WIDTH),
                                 lambda b, i: (b, jnp.maximum(i * hstep - 1, 0), 0))
    else:
        halo = st16
        halo_spec = pl.BlockSpec((bb, POOL_HALO, POOL_WIDTH), lambda b, i: (b, 0, 0))
    return pl.pallas_call(
        functools.partial(_pool_kernel, pos0),
        grid=(B // bb, nl),
        in_specs=[pl.BlockSpec((bb, tl, POOL_WIDTH), lambda b, i: (b, i, 0)),
                  halo_spec,
                  pl.BlockSpec((bb, POOL_HALO, POOL_WIDTH), lambda b, i: (b, 0, 0)),
                  pl.BlockSpec((len(POOL_WINDOWS), POOL_GROUP, POOL_GROUP), lambda b, i: (0, 0, 0)),
                  pl.BlockSpec((1, POOL_WIDTH), lambda b, i: (0, 0))],
        out_specs=[pl.BlockSpec((bb * tl, POOL_WIDTH), lambda b, i: (b * nl + i, 0)),
                   pl.BlockSpec((bb, POOL_BUF, POOL_WIDTH), lambda b, i: (b, 0, 0))],
        out_shape=[jax.ShapeDtypeStruct((B * L, POOL_WIDTH), BF16),
                   jax.ShapeDtypeStruct((B, POOL_BUF, POOL_WIDTH), F32)],
        scratch_shapes=[pltpu.VMEM((bb, POOL_HALO + tl, POOL_WIDTH), F32)],
        compiler_params=_params("arbitrary", "arbitrary"),
        name="pool_mixer",
    )(p, halo, st16, w_pool, scale)


def _seg_mats():
    ch = np.arange(RWKV_WIDTH) // HEAD_DIM
    ln = np.arange(LANES)
    seg = (ch[:, None] == (ln[None, :] % N_HEADS)).astype(np.float32)
    exp = ((ln[:, None] == ch[None, :])).astype(np.float32)
    stack3 = lambda m: jnp.asarray(np.concatenate([m, m, m], axis=0), dtype=BF16)
    return stack3(seg), stack3(exp)


def _pre_kernel(z_ref, zp_ref, st_ref, mu_ref, w0_ref, w2_ref, a0_ref, a2_ref, g2_ref,
                kk_ref, ka_ref, rk_ref, seg_ref, exp_ref,
                aq_o, wb_o, kk_o, v_o, br_o, kr_o, bs_o, g_o):
    bb, tl, _ = z_ref.shape
    i = pl.program_id(1)
    z = z_ref[...]
    first = jnp.where(i == 0, st_ref[...], zp_ref[:, SUBLANES - 1:SUBLANES, :])
    tpos = lax.broadcasted_iota(jnp.int32, (1, tl, LANES), 1)
    n = bb * tl

    def shifted(c0, c1):
        zc = z[:, :, c0:c1]
        prev = pltpu.roll(zc, 1, axis=1)
        reps = (c1 - c0) // LANES
        t0 = jnp.concatenate([tpos] * reps, axis=2) if reps > 1 else tpos
        prev = jnp.where(t0 == 0, first[:, :, c0:c1], prev)
        return (zc + (prev - zc) * mu_ref[:, c0:c1]).reshape(n, c1 - c0)

    W = RWKV_WIDTH
    r = shifted(0, W)
    k = shifted(W, 2 * W)
    v = shifted(2 * W, 3 * W)
    lora = shifted(LORA_OFF, LORA_OFF + LANES)
    gd = shifted(GATE_OFF, Z_WIDTH)

    w_lin = w0_ref[...] + jnp.dot(jnp.tanh(lora).astype(BF16), w2_ref[...], preferred_element_type=F32)
    a_lin = a0_ref[...] + jnp.dot(lora.astype(BF16), a2_ref[...], preferred_element_type=F32)
    g_o[...] = jnp.dot(_sigmoid(gd).astype(BF16), g2_ref[...], preferred_element_type=F32)
    w = -(jnp.maximum(-w_lin, 0.0) + jnp.log(1.0 + jnp.exp(-jnp.abs(w_lin)))) - 0.5
    decay = jnp.exp(-jnp.exp(w))
    a_sig = _sigmoid(a_lin)

    def select_dot(x, sel3_ref):
        hi = x.astype(BF16)
        r1 = x - hi.astype(F32)
        mid = r1.astype(BF16)
        lo = (r1 - mid.astype(F32)).astype(BF16)
        return jnp.dot(jnp.concatenate([hi, mid, lo], axis=1), sel3_ref[...], preferred_element_type=F32)

    def head_sum(x):
        return select_dot(x, seg_ref)

    kk = k * kk_ref[...]
    inv = 1.0 / jnp.maximum(jnp.sqrt(head_sum(kk * kk)), 1e-12)
    kk = kk * select_dot(inv, exp_ref)
    k2 = k * (1.0 + (a_sig - 1.0) * ka_ref[...])
    b = kk * a_sig

    low_half = lax.broadcasted_iota(jnp.int32, (n, LANES), 1) < HEAD_DIM

    def store_pair(o_ref, xa, xb):
        for c in range(RWKV_WIDTH // LANES):
            cols = slice(c * LANES, (c + 1) * LANES)
            ac, bc = xa[:, cols], xb[:, cols]
            a_sw = pltpu.roll(ac, HEAD_DIM, axis=1)
            b_sw = pltpu.roll(bc, HEAD_DIM, axis=1)
            o_ref[pl.ds(2 * c, n, stride=N_HEADS), :] = jnp.where(low_half, ac, b_sw)
            o_ref[pl.ds(2 * c + 1, n, stride=N_HEADS), :] = jnp.where(low_half, a_sw, bc)

    store_pair(aq_o, -kk, decay * r)
    store_pair(wb_o, decay, b)
    store_pair(kk_o, k2, k2)
    for c in range(RWKV_WIDTH // LANES):
        v_o[pl.ds(c, n, stride=SUBLANES), :] = v[:, c * LANES:(c + 1) * LANES]
    br_o[...] = head_sum(b * r)
    kr_o[...] = head_sum(k2 * r)
    bs_o[...] = head_sum(r * k2 * rk_ref[...])


def _rwkv_pre(z, st_shift, lw):
    B, L, _ = z.shape
    bb, tl = _tiles(B, L)
    nl = L // tl
    pstep = tl // SUBLANES
    seg, exp = _seg_mats()
    row = lambda wdt: pl.BlockSpec((1, wdt), lambda b, i: (0, 0))
    full = lambda a: pl.BlockSpec(a.shape, lambda b, i: (0,) * a.ndim)
    out_w = pl.BlockSpec((bb * tl, RWKV_WIDTH), lambda b, i: (b * nl + i, 0))
    out_s = pl.BlockSpec((bb * tl, LANES), lambda b, i: (b * nl + i, 0))
    out_p = pl.BlockSpec((bb * tl * N_HEADS, LANES), lambda b, i: (b * nl + i, 0))
    out_v = pl.BlockSpec((bb * tl * SUBLANES, LANES), lambda b, i: (b * nl + i, 0))
    T = B * L
    wide = jax.ShapeDtypeStruct((T, RWKV_WIDTH), F32)
    pair = jax.ShapeDtypeStruct((T * N_HEADS, LANES), F32)
    folded = jax.ShapeDtypeStruct((T * SUBLANES, LANES), F32)
    small = jax.ShapeDtypeStruct((T, LANES), F32)
    return pl.pallas_call(
        _pre_kernel,
        grid=(B // bb, nl),
        in_specs=[pl.BlockSpec((bb, tl, Z_WIDTH), lambda b, i: (b, i, 0)),
                  pl.BlockSpec((bb, SUBLANES, Z_WIDTH), lambda b, i: (b, jnp.maximum(i * pstep - 1, 0), 0)),
                  pl.BlockSpec((bb, 1, Z_WIDTH), lambda b, i: (b, 0, 0)),
                  row(Z_WIDTH), row(RWKV_WIDTH), full(lw["w2"]), row(RWKV_WIDTH), full(lw["a2"]),
                  full(lw["g2"]), row(RWKV_WIDTH), row(RWKV_WIDTH), row(RWKV_WIDTH), full(seg), full(exp)],
        out_specs=[out_p] * 3 + [out_v] + [out_s] * 3 + [out_w],
        out_shape=[pair] * 3 + [folded] + [small] * 3 + [wide],
        compiler_params=_params("arbitrary", "arbitrary"),
        name="rwkv_pre",
    )(z, z, st_shift, lw["mu"], lw["w0"], lw["w2"], lw["a0"], lw["a2"], lw["g2"],
      lw["k_k"], lw["k_a"], lw["r_k"], seg, exp)


N_ACC = 4


def _scan_kernel(aq_ref, wb_ref, kk_ref, v_ref, br_ref, kr_ref, s0_ref, y_ref, s_ref, op_tiles, op_first):
    tc = v_ref.shape[0]
    rows = SUBLANES * N_HEADS
    n_groups = tc // SUBLANES

    @pl.when(pl.program_id(1) == 0)
    def _():
        s_ref[...] = s0_ref[...]

    def keys_to_sublanes(blk):
        return jnp.concatenate([blk] * SUBLANES, axis=0).T

    def put_pair(op_ref, slot, first, tile):
        op_ref[slot, first] = tile[0:HEAD_DIM]
        op_ref[slot, first + 1] = tile[HEAD_DIM:]

    def tree_sum(parts):
        while len(parts) > 1:
            parts = [parts[i] + parts[i + 1] for i in range(0, len(parts), 2)]
        return parts[0]

    def reduce_keys(op_ref, slot, state_of):
        sa_p = [None] * N_ACC
        yq_p = [None] * N_ACC
        for j in range(HEAD_DIM):
            sj = state_of(j)
            pa = sj * op_ref[slot, 0, pl.ds(j, 1), :]
            pq = sj * op_ref[slot, 1, pl.ds(j, 1), :]
            i = j % N_ACC
            sa_p[i] = pa if sa_p[i] is None else sa_p[i] + pa
            yq_p[i] = pq if yq_p[i] is None else yq_p[i] + pq
        return tree_sum(sa_p), tree_sum(yq_p)

    def prepare(g, op_ref):
        r0 = pl.multiple_of(jnp.minimum(g, n_groups - 1) * rows, rows)
        r_next = pl.multiple_of(jnp.minimum(g + 1, n_groups - 1) * rows, rows)
        aq = aq_ref[pl.ds(r0, rows), :]
        wb = wb_ref[pl.ds(r0, rows), :]
        kx = kk_ref[pl.ds(r0, rows), :]
        aq_next = aq_ref[pl.ds(r_next, N_HEADS), :]
        for t in range(SUBLANES):
            nxt = aq[(t + 1) * N_HEADS:(t + 2) * N_HEADS] if t + 1 < SUBLANES else aq_next
            put_pair(op_ref, t, 0, keys_to_sublanes(nxt))
            put_pair(op_ref, t, 2, keys_to_sublanes(wb[t * N_HEADS:(t + 1) * N_HEADS]))
            op_ref[t, 4] = keys_to_sublanes(kx[t * N_HEADS:(t + 1) * N_HEADS])[0:HEAD_DIM]

    def run_group(g, carry, op_ref):
        sa, yq = carry
        for t in range(SUBLANES):
            step = g * SUBLANES + t
            v = v_ref[step]
            y_ref[step] = yq + sa * br_ref[pl.ds(step, 1), :] + v * kr_ref[pl.ds(step, 1), :]

            def updated(j, t=t, sa=sa, v=v):
                sj = (s_ref[0, j] * op_ref[t, 2, pl.ds(j, 1), :] + sa * op_ref[t, 3, pl.ds(j, 1), :]
                      + v * op_ref[t, 4, pl.ds(j, 1), :])
                s_ref[0, j] = sj
                return sj

            sa, yq = reduce_keys(op_ref, t, updated)
        return sa, yq

    put_pair(op_first, 0, 0, keys_to_sublanes(aq_ref[0:N_HEADS, :]))
    first = reduce_keys(op_first, 0, lambda j: s_ref[0, j])

    def one_group(g, carry):
        prepare(g, op_tiles)
        return run_group(g, carry, op_tiles)

    lax.fori_loop(0, n_groups, one_group, first)


def _scan(ops, s0, B, L):
    aq, wb, kk, v, br, kr = ops
    tc = min(L, 128)
    nc = L // tc
    T = B * L
    nblk = RWKV_WIDTH // LANES
    op_spec = pl.BlockSpec((tc * N_HEADS, LANES), lambda bi, c: (bi * nc + c, 0))
    v_spec = pl.BlockSpec((tc, nblk, LANES), lambda bi, c: (bi * nc + c, 0, 0))
    s_spec = pl.BlockSpec((tc, LANES), lambda bi, c: (bi * nc + c, 0))
    st_spec = pl.BlockSpec((1, HEAD_DIM, nblk, LANES), lambda bi, c: (bi, 0, 0, 0))
    y, s = pl.pallas_call(
        _scan_kernel,
        grid=(B, nc),
        in_specs=[op_spec] * 3 + [v_spec, s_spec, s_spec, st_spec],
        out_specs=[v_spec, st_spec],
        out_shape=[jax.ShapeDtypeStruct((T, nblk, LANES), F32),
                   jax.ShapeDtypeStruct((B, HEAD_DIM, nblk, LANES), F32)],
        scratch_shapes=[pltpu.VMEM((SUBLANES, 5, HEAD_DIM, LANES), F32),
                        pltpu.VMEM((1, 2, HEAD_DIM, LANES), F32)],
        compiler_params=_params("arbitrary", "arbitrary"),
        name="rwkv_scan",
    )(aq, wb, kk, v.reshape(T, nblk, LANES), br, kr, s0)
    return y.reshape(T * nblk, LANES), s


def _post_kernel(y_ref, v_ref, g_ref, bs_ref, yp_ref, x_ref, lw_ref, lb_ref, wo_ref,
                 gta_ref, n2_ref, sh_ref, sc_ref, wr_ref, x1_ref, h2_ref, s_ref):
    bb, tl, _ = x_ref.shape
    nblk = RWKV_WIDTH // LANES
    n = bb * tl
    cols = [slice(c * LANES, (c + 1) * LANES) for c in range(nblk)]
    folded = lambda ref, c: ref[pl.ds(c, n, stride=nblk), :]
    y = [folded(y_ref, c) for c in range(nblk)]

    def head_total(parts):
        s = parts[0]
        for part in parts[1:]:
            s = s + part
        for sh in (N_HEADS, 2 * N_HEADS, 4 * N_HEADS):
            s = s + pltpu.roll(s, sh, axis=1)
        return s

    mean = head_total(y) * (1.0 / HEAD_DIM)
    dev = [yc - mean for yc in y]
    var = head_total([d * d for d in dev]) * (1.0 / HEAD_DIM)
    rstd = lax.rsqrt(var + LNX_EPS)
    bs = bs_ref[...]
    outs = []
    for ci, c in enumerate(cols):
        yn = dev[ci] * rstd * lw_ref[:, c] + lb_ref[:, c]
        outs.append(((yn + bs * folded(v_ref, ci)) * g_ref[:, c]).astype(BF16))
    yr = jnp.concatenate(outs, axis=1)
    mix = jnp.dot(yp_ref[...], wo_ref[0:POOL_WIDTH, :], preferred_element_type=F32)
    mix = mix + jnp.dot(yr, wo_ref[POOL_WIDTH:, :], preferred_element_type=F32)
    x1 = x_ref[...] + gta_ref[...] * mix.reshape(bb, tl, D_MODEL)
    x1_ref[...] = x1
    h2 = _ada_norm_val(x1, n2_ref[...], sh_ref[...], sc_ref[...]).reshape(bb * tl, D_MODEL)
    h2_ref[...] = h2
    s_ref[...] = _sigmoid(jnp.dot(h2, wr_ref[...], precision=HIGHEST, preferred_element_type=F32))


def _post(y, v, g, bs, ypool, x, lw, mod, l, w_router_p):
    B, L, _ = x.shape
    bb, tl = _tiles(B, L)
    nl = L // tl
    T = B * L
    rows = lambda wdt: pl.BlockSpec((bb * tl, wdt), lambda b, i: (b * nl + i, 0))
    fold_rows = pl.BlockSpec((bb * tl * RWKV_WIDTH // LANES, LANES), lambda b, i: (b * nl + i, 0))
    row = lambda wdt: pl.BlockSpec((1, wdt), lambda b, i: (0, 0))
    xs = pl.BlockSpec((bb, tl, D_MODEL), lambda b, i: (b, i, 0))
    return pl.pallas_call(
        _post_kernel,
        grid=(B // bb, nl),
        in_specs=[fold_rows, fold_rows, rows(RWKV_WIDTH), rows(LANES), rows(POOL_WIDTH), xs,
                  row(RWKV_WIDTH), row(RWKV_WIDTH),
                  pl.BlockSpec((D_MODEL, D_MODEL), lambda b, i: (0, 0)),
                  _mod_spec(bb, l, 2), row(D_MODEL), _mod_spec(bb, l, 3), _mod_spec(bb, l, 4),
                  pl.BlockSpec((D_MODEL, LANES), lambda b, i: (0, 0))],
        out_specs=[xs, rows(D_MODEL), rows(LANES)],
        out_shape=[jax.ShapeDtypeStruct((B, L, D_MODEL), F32),
                   jax.ShapeDtypeStruct((T, D_MODEL), F32),
                   jax.ShapeDtypeStruct((T, LANES), F32)],
        compiler_params=_params("arbitrary", "arbitrary"),
        name="post_outproj_norm2",
    )(y, v, g, bs, ypool, x, lw["lnx_w"], lw["lnx_b"], lw["w_out"], mod, lw["norm2_g"], mod, mod, w_router_p)


def _route(scores, b_router):
    T = scores.shape[0]
    sel = scores + b_router.astype(F32)
    neg = jnp.float32(-jnp.inf)

    def top2(x):
        lane = lax.broadcasted_iota(jnp.int32, x.shape, x.ndim - 1)
        i0 = jnp.argmax(x, -1)
        m0 = jnp.max(x, -1)
        rest = jnp.where(lane == i0[..., None], neg, x)
        return m0, jnp.max(rest, -1), i0, jnp.argmax(rest, -1)

    g0, g1, _, _ = top2(sel.reshape(T, N_EXPERT_GROUPS, EXPERTS_PER_GROUP))
    gbest = jnp.argmax(g0 + g1, -1)
    in_group = (jnp.arange(N_EXPERTS) // EXPERTS_PER_GROUP)[None, :] == gbest[:, None]
    _, _, i0, i1 = top2(jnp.where(in_group, sel, neg))
    idx = jnp.stack([i0, i1], axis=1).astype(jnp.int32)
    wts = jnp.take_along_axis(scores, idx, 1)
    wts = wts / jnp.sum(wts, -1, keepdims=True)
    return idx, wts


def _moe_rows(T):
    return -(-(T * TOP_K + N_EXPERTS * (MOE_BLOCK - 1)) // MOE_BLOCK)


def _dispatch_plan(idx):
    T = idx.shape[0]
    A = T * TOP_K
    nb = _moe_rows(T)
    e_flat = idx.reshape(A)
    experts = jnp.arange(N_EXPERTS, dtype=jnp.int32)
    onehot = (e_flat[:, None] == experts[None, :]).astype(jnp.int32)
    csum = jnp.cumsum(onehot, axis=0)
    rank = jnp.sum(csum * onehot, axis=1) - 1
    counts = csum[-1]
    padded = (counts + MOE_BLOCK - 1) // MOE_BLOCK * MOE_BLOCK
    pend = jnp.cumsum(padded)
    pstart = pend - padded
    dest = (jnp.sum(onehot * pstart[None, :], axis=1) + rank).astype(jnp.int32)
    blk0 = jnp.arange(nb, dtype=jnp.int32) * MOE_BLOCK
    block_e = jnp.sum((pend[None, :] <= blk0[:, None]).astype(jnp.int32), axis=1)
    block_e = jnp.minimum(block_e, N_EXPERTS - 1).astype(jnp.int32)
    n_valid = (pend[-1] // MOE_BLOCK).astype(jnp.int32).reshape(1)
    pad_start = jnp.concatenate([pstart + counts, pend[-1:]]).astype(jnp.int32)
    pad_cnt = jnp.concatenate([padded - counts, nb * MOE_BLOCK - pend[-1:]]).astype(jnp.int32)
    return dest, block_e, n_valid, pad_start, pad_cnt


def _dispatch_kernel(tiles_a, dest_ref, pad_start_ref, pad_cnt_ref, ha_ref, hb_ref, xs_ref, sem):
    i = pl.program_id(0)
    base = i * (TOP_K * ROW_TILE)

    def row_copy(h_ref, src_row, dst_row):
        return pltpu.make_async_copy(h_ref.at[pl.ds(src_row, 1), :], xs_ref.at[pl.ds(dst_row, 1), :], sem)

    def scatter_tile(h_ref):
        def issue(r, carry):
            for k in range(TOP_K):
                row_copy(h_ref, r, dest_ref[base + TOP_K * r + k]).start(priority=k)
            return carry

        def drain(r, carry):
            for k in range(TOP_K):
                row_copy(h_ref, 0, 0).wait()
            return carry

        lax.fori_loop(0, ROW_TILE, issue, 0, unroll=8)
        lax.fori_loop(0, ROW_TILE, drain, 0, unroll=8)

    @pl.when(i < tiles_a)
    def _():
        scatter_tile(ha_ref)

    @pl.when(i >= tiles_a)
    def _():
        scatter_tile(hb_ref)

    def fill(e, wait):
        start, cnt = pad_start_ref[e], pad_cnt_ref[e]
        go = (lambda cp: cp.wait()) if wait else (lambda cp: cp.start())
        head = jnp.minimum((-start) & (SUBLANES - 1), cnt)

        def chunk(size, at):
            at = pl.multiple_of(at, SUBLANES)
            return pltpu.make_async_copy(ha_ref.at[pl.ds(0, size), :], xs_ref.at[pl.ds(at, size), :], sem)

        def one(j, carry):
            go(row_copy(ha_ref, 0, start + j))
            return carry

        lax.fori_loop(0, head, one, 0)
        pos = start + head
        rem = cnt - head

        def full(j, carry):
            go(chunk(ROW_TILE, pos + j * ROW_TILE))
            return carry

        n_full = rem // ROW_TILE
        lax.fori_loop(0, n_full, full, 0)
        pos = pos + n_full * ROW_TILE
        size = ROW_TILE // 2
        while size >= SUBLANES:
            take = (rem & size) != 0

            @pl.when(take)
            def _(size=size, pos=pos):
                go(chunk(size, pos))

            pos = pos + jnp.where(take, size, 0)
            size //= 2

    @pl.when(i == 0)
    def _():
        for e in range(N_EXPERTS + 1):
            fill(e, False)
        for e in range(N_EXPERTS + 1):
            fill(e, True)


def _dispatch_rows(ha, hb, dest, pad_start, pad_cnt, n_rows):
    tiles_a, tiles_b = ha.shape[0] // ROW_TILE, hb.shape[0] // ROW_TILE
    return pl.pallas_call(
        functools.partial(_dispatch_kernel, tiles_a),
        grid_spec=pltpu.PrefetchScalarGridSpec(
            num_scalar_prefetch=3,
            grid=(tiles_a + tiles_b,),
            in_specs=[pl.BlockSpec((ROW_TILE, D_MODEL), lambda i, *_: (jnp.minimum(i, tiles_a - 1), 0)),
                      pl.BlockSpec((ROW_TILE, D_MODEL), lambda i, *_: (jnp.maximum(i - tiles_a, 0), 0))],
            out_specs=pl.BlockSpec(memory_space=pl.ANY),
            scratch_shapes=[pltpu.SemaphoreType.DMA(())]),
        out_shape=jax.ShapeDtypeStruct((n_rows, D_MODEL), F32),
        compiler_params=_params("arbitrary"),
        name="moe_dispatch",
    )(dest, pad_start, pad_cnt, ha, hb)


def _expert_changed(be_ref, b):
    return jnp.logical_or(b == 0, be_ref[b] != be_ref[jnp.maximum(b - 1, 0)])


def _gateup_kernel(be_ref, nv_ref, x_ref, wg_ref, wu_ref, o_ref, wgc, wuc):
    b = pl.program_id(1)

    @pl.when(_expert_changed(be_ref, b))
    def _():
        wgc[...] = wg_ref[0, 0].astype(BF16)
        wuc[...] = wu_ref[0, 0].astype(BF16)

    @pl.when(b < nv_ref[0])
    def _():
        x = x_ref[...].astype(BF16)
        gate = jnp.dot(x, wgc[...], preferred_element_type=F32)
        up = jnp.dot(x, wuc[...], preferred_element_type=F32)
        o_ref[...] = (gate * _sigmoid(gate) * up).astype(BF16)

    @pl.when(b >= nv_ref[0])
    def _():
        o_ref[...] = jnp.zeros_like(o_ref)


def _down_kernel(be_ref, nv_ref, a_ref, wd_ref, o_ref, wdc):
    b = pl.program_id(1)

    @pl.when(_expert_changed(be_ref, b))
    def _():
        wdc[...] = wd_ref[0, 0].astype(BF16)

    @pl.when(b < nv_ref[0])
    def _():
        o_ref[...] = jnp.dot(a_ref[...], wdc[...], preferred_element_type=F32)

    @pl.when(b >= nv_ref[0])
    def _():
        o_ref[...] = jnp.zeros_like(o_ref)


def _moe_matmuls(l, xs, block_e, n_valid, w_gate, w_up, w_down):
    R = xs.shape[0]
    nb = R // MOE_BLOCK
    ce = D_EXPERT
    live = lambda b, nv: jnp.minimum(b, nv[0] - 1)
    act = pl.pallas_call(
        _gateup_kernel,
        grid_spec=pltpu.PrefetchScalarGridSpec(
            num_scalar_prefetch=2,
            grid=(D_EXPERT // ce, nb),
            in_specs=[pl.BlockSpec((MOE_BLOCK, D_MODEL), lambda c, b, be, nv: (live(b, nv), 0)),
                      pl.BlockSpec((1, 1, D_MODEL, ce), lambda c, b, be, nv: (l, be[b], 0, c)),
                      pl.BlockSpec((1, 1, D_MODEL, ce), lambda c, b, be, nv: (l, be[b], 0, c))],
            out_specs=pl.BlockSpec((MOE_BLOCK, ce), lambda c, b, be, nv: (b, c)),
            scratch_shapes=[pltpu.VMEM((D_MODEL, ce), BF16), pltpu.VMEM((D_MODEL, ce), BF16)]),
        out_shape=jax.ShapeDtypeStruct((R, D_EXPERT), BF16),
        compiler_params=_params("arbitrary", "arbitrary"),
        name="moe_gate_up",
    )(block_e, n_valid, xs, w_gate, w_up)
    cd = D_MODEL
    return pl.pallas_call(
        _down_kernel,
        grid_spec=pltpu.PrefetchScalarGridSpec(
            num_scalar_prefetch=2,
            grid=(D_MODEL // cd, nb),
            in_specs=[pl.BlockSpec((MOE_BLOCK, D_EXPERT), lambda c, b, be, nv: (live(b, nv), 0)),
                      pl.BlockSpec((1, 1, D_EXPERT, cd), lambda c, b, be, nv: (l, be[b], 0, c))],
            out_specs=pl.BlockSpec((MOE_BLOCK, cd), lambda c, b, be, nv: (b, c)),
            scratch_shapes=[pltpu.VMEM((D_EXPERT, cd), BF16)]),
        out_shape=jax.ShapeDtypeStruct((R, D_MODEL), F32),
        compiler_params=_params("arbitrary", "arbitrary"),
        name="moe_down",
    )(block_e, n_valid, act, w_down)


def _combine_kernel(final, pos_ref, x_ref, w0_ref, w1_ref, gt_ref, fg_ref, yb_ref, o_ref, y0_buf, y1_buf, sem):
    bb, tl, _ = x_ref.shape
    tile = pl.program_id(0) * pl.num_programs(1) + pl.program_id(1)
    base = tile * (TOP_K * ROW_TILE)

    def row_copy(src_row, buf, r):
        return pltpu.make_async_copy(yb_ref.at[pl.ds(src_row, 1), :], buf.at[pl.ds(r, 1), :], sem)

    def issue(r, carry):
        row_copy(pos_ref[base + TOP_K * r], y0_buf, r).start(priority=0)
        row_copy(pos_ref[base + TOP_K * r + 1], y1_buf, r).start(priority=1)
        return carry

    def drain(r, carry):
        row_copy(0, y0_buf, 0).wait()
        row_copy(0, y1_buf, 0).wait()
        return carry

    lax.fori_loop(0, ROW_TILE, issue, 0, unroll=8)
    lax.fori_loop(0, ROW_TILE, drain, 0, unroll=8)

    w0 = w0_ref[...]
    w1 = w1_ref[...]
    for c in range(D_MODEL // LANES):
        cols = slice(c * LANES, (c + 1) * LANES)
        ff = y0_buf[:, cols] * w0 + y1_buf[:, cols] * w1
        o_ref[:, :, cols] = x_ref[:, :, cols] + gt_ref[:, :, cols] * ff.reshape(bb, tl, LANES)
    if final:
        x2 = o_ref[...]
        ms = jnp.mean(x2 * x2, axis=-1, keepdims=True)
        o_ref[...] = x2 * lax.rsqrt(ms + RMS_EPS) * fg_ref[...]


def _combine(x1, yb, pos, w0b, w1b, mod, l, final_g, final):
    B, L, _ = x1.shape
    bb, tl = _tiles(B, L)
    nl = L // tl
    xs = pl.BlockSpec((bb, tl, D_MODEL), lambda b, i, *_: (b, i, 0))
    rows = pl.BlockSpec((bb * tl, LANES), lambda b, i, *_: (b * nl + i, 0))
    return pl.pallas_call(
        functools.partial(_combine_kernel, final),
        grid_spec=pltpu.PrefetchScalarGridSpec(
            num_scalar_prefetch=1,
            grid=(B // bb, nl),
            in_specs=[xs, rows, rows, _mod_spec(bb, l, 5),
                      pl.BlockSpec((1, D_MODEL), lambda b, i, *_: (0, 0)),
                      pl.BlockSpec(memory_space=pl.ANY)],
            out_specs=xs,
            scratch_shapes=[pltpu.VMEM((ROW_TILE, D_MODEL), F32), pltpu.VMEM((ROW_TILE, D_MODEL), F32),
                            pltpu.SemaphoreType.DMA(())]),
        out_shape=jax.ShapeDtypeStruct((B, L, D_MODEL), F32),
        compiler_params=_params("arbitrary", "arbitrary"),
        name="moe_combine",
    )(pos, x1, w0b, w1b, mod, final_g, yb)


def _vh(x, axis=-1):
    axis = axis % x.ndim
    split = x.reshape(x.shape[:axis] + (N_HEADS, HEAD_DIM) + x.shape[axis + 1:])
    return jnp.swapaxes(split, axis, axis + 1).reshape(x.shape)


def _vh_inverse(x, axis=-1):
    axis = axis % x.ndim
    split = x.reshape(x.shape[:axis] + (HEAD_DIM, N_HEADS) + x.shape[axis + 1:])
    return jnp.swapaxes(split, axis, axis + 1).reshape(x.shape)


def _layer_weights(l, w_in, w_pool, pool_scale, mu_shift, w0, w2, a0, a2, g2, k_k, k_a, r_k,
                   lnx_w, lnx_b, w_out, norm1_g, norm2_g):
    W = RWKV_WIDTH
    wz = w_in[l][:, POOL_WIDTH:]
    wz = jnp.concatenate([wz[:, :2 * W], _vh(wz[:, 2 * W:3 * W]), wz[:, 3 * W:],
                          jnp.zeros((D_MODEL, Z_WIDTH - SHIFT_WIDTH), F32)], axis=1).astype(BF16)
    mu = mu_shift[l]
    mu = jnp.concatenate([mu[:2 * W], _vh(mu[2 * W:3 * W]), mu[3 * W:],
                          jnp.zeros((Z_WIDTH - SHIFT_WIDTH,), F32)]).reshape(1, Z_WIDTH)
    zpad = lambda m, top, tot: jnp.concatenate(
        [jnp.zeros((top, W), F32), m, jnp.zeros((tot - top - m.shape[0], W), F32)], axis=0).astype(BF16)
    wo = w_out[l]
    wo = jnp.concatenate([wo[:POOL_WIDTH], _vh(wo[POOL_WIDTH:], axis=0)], axis=0).astype(BF16)
    r1 = lambda x: x.reshape(1, -1)
    return dict(
        wp=w_in[l][:, :POOL_WIDTH].astype(BF16), wz=wz, mu=mu,
        w_pool=w_pool[l].astype(BF16), pool_scale=r1(pool_scale[l]),
        w0=r1(w0[l]), w2=zpad(w2[l], 0, LANES), a0=r1(a0[l]), a2=zpad(a2[l], DECAY_LORA, LANES),
        g2=zpad(_vh(g2[l]), 0, Z_WIDTH - GATE_OFF),
        k_k=r1(k_k[l]), k_a=r1(k_a[l]), r_k=r1(r_k[l]),
        lnx_w=r1(_vh(lnx_w[l])), lnx_b=r1(_vh(lnx_b[l])), w_out=wo,
        norm1_g=r1(norm1_g[l]), norm2_g=r1(norm2_g[l]))


def _shift_to_z(s):
    W = RWKV_WIDTH
    out = jnp.concatenate([s[:, :2 * W], _vh(s[:, 2 * W:3 * W]), s[:, 3 * W:],
                           jnp.zeros((s.shape[0], Z_WIDTH - SHIFT_WIDTH), s.dtype)], axis=1)
    return out[:, None, :]


def _z_to_shift(zrow):
    W = RWKV_WIDTH
    v = _vh_inverse(zrow[:, 2 * W:3 * W])
    return jnp.concatenate([zrow[:, :2 * W], v, zrow[:, 3 * W:SHIFT_WIDTH]], axis=1)


def _state_to_tile(s):
    B = s.shape[0]
    return s.transpose(0, 3, 2, 1).reshape(B, HEAD_DIM, RWKV_WIDTH // LANES, LANES)


def _tile_to_state(p):
    B = p.shape[0]
    return p.reshape(B, HEAD_DIM, HEAD_DIM, N_HEADS).transpose(0, 3, 2, 1)


def _mix_half(l, x, mod, st_wkv, st_pool, st_shift, pos0, lw, w_router_p):
    B, L, _ = x.shape
    p, z = _inproj(x, lw["norm1_g"], mod, l, lw["wp"], lw["wz"])
    st16 = jnp.concatenate([jnp.zeros((B, 1, POOL_WIDTH), F32), st_pool], axis=1)
    ypool, buf = _pool(p, st16, lw["w_pool"], lw["pool_scale"], pos0)
    aq, wb, kk, v, br, kr, bs, g = _rwkv_pre(z, _shift_to_z(st_shift), lw)
    y, s_fin = _scan((aq, wb, kk, v, br, kr), _state_to_tile(st_wkv), B, L)
    x1, h2, scores = _post(y, v, g, bs, ypool, x, lw, mod, l, w_router_p)
    return x1, h2, scores, _tile_to_state(s_fin), buf, _z_to_shift(z[:, -1])


def kernel(x_prompt, x_sample, state_wkv, state_pool, state_shift, c_prompt, c_sample, w_ada, b_ada, norm1_g, norm2_g, w_in, w_pool, pool_scale, mu_shift, w0, w2, a0, a2, g2, k_k, k_a, r_k, lnx_w, lnx_b, w_out, w_router, b_router, w_gate, w_up, w_down, final_g):
    Bp, Lp, _ = x_prompt.shape
    Bs, Ls, _ = x_sample.shape
    Tp, Ts = Bp * Lp, Bs * Ls
    mod = _ada(jnp.concatenate([c_prompt, c_sample], axis=0), w_ada, b_ada)
    mod = mod.reshape(DEPTH, Bp + Bs, 6, 1, D_MODEL).transpose(0, 2, 1, 3, 4)
    mod_p, mod_s = mod[:, :, :Bp], mod[:, :, Bp:]
    w_router_p = jnp.concatenate([w_router, jnp.zeros((D_MODEL, LANES - N_EXPERTS), F32)], axis=1)
    fg = final_g.reshape(1, D_MODEL)
    n_rows = _moe_rows(Tp + Ts) * MOE_BLOCK

    xp, xs_ = x_prompt, x_sample
    outs_p, outs_s = [], []
    for l in range(DEPTH):
        lw = _layer_weights(l, w_in, w_pool, pool_scale, mu_shift, w0, w2, a0, a2, g2, k_k, k_a, r_k,
                            lnx_w, lnx_b, w_out, norm1_g, norm2_g)
        zp = lambda *s: jnp.zeros(s, F32)
        x1p, h2p, scp, wkv_p, buf_p, shf_p = _mix_half(
            l, xp, mod_p, zp(Bp, N_HEADS, HEAD_DIM, HEAD_DIM), zp(Bp, POOL_BUF, POOL_WIDTH),
            zp(Bp, SHIFT_WIDTH), 0, lw, w_router_p)
        x1s, h2s, scs, wkv_s, buf_s, shf_s = _mix_half(
            l, xs_, mod_s, state_wkv[l], state_pool[l], state_shift[l], PAST_LEN, lw, w_router_p)
        outs_p.append((wkv_p, buf_p, shf_p))
        outs_s.append((wkv_s, buf_s, shf_s))

        scores = jnp.concatenate([scp, scs], axis=0)[:, :N_EXPERTS]
        idx, wts = _route(scores, b_router)
        dest, block_e, n_valid, pad_start, pad_cnt = _dispatch_plan(idx)
        xs_rows = _dispatch_rows(h2p, h2s, dest, pad_start, pad_cnt, n_rows)
        yb = _moe_matmuls(l, xs_rows, block_e, n_valid, w_gate, w_up, w_down)
        w0b = jnp.broadcast_to(wts[:, 0:1], (Tp + Ts, LANES))
        w1b = jnp.broadcast_to(wts[:, 1:2], (Tp + Ts, LANES))
        final = l == DEPTH - 1
        xp = _combine(x1p, yb, dest[:TOP_K * Tp], w0b[:Tp], w1b[:Tp], mod_p, l, fg, final)
        xs_ = _combine(x1s, yb, dest[TOP_K * Tp:], w0b[Tp:], w1b[Tp:], mod_s, l, fg, final)

    st = lambda outs, i: jnp.stack([o[i] for o in outs])
    return (xp, xs_, st(outs_p, 0), st(outs_p, 1), st(outs_p, 2),
            st(outs_s, 0), st(outs_s, 1), st(outs_s, 2))
```

```python
import functools

import numpy as np
import jax
import jax.numpy as jnp
from jax import lax
from jax.experimental import pallas as pl
from jax.experimental.pallas import tpu as pltpu

F32 = jnp.float32
BF16 = jnp.bfloat16

D_MODEL = 2048
DEPTH = 4
PAST_LEN = 16384
POOL_WIDTH = 1024
POOL_WINDOWS = (2, 4, 8, 16)
POOL_GROUP = 256
POOL_BUF = 15
POOL_HALO = 16
RWKV_WIDTH = 1024
HEAD_DIM = 64
N_HEADS = 16
DECAY_LORA = 64
AAA_LORA = 64
GATE_LORA = 160
SHIFT_WIDTH = 3 * RWKV_WIDTH + DECAY_LORA + AAA_LORA + GATE_LORA
Z_WIDTH = 3456
LORA_OFF = 3 * RWKV_WIDTH
GATE_OFF = LORA_OFF + 128
LNX_EPS = 64e-5
RMS_EPS = 1e-6
N_EXPERTS = 16
N_EXPERT_GROUPS = 4
EXPERTS_PER_GROUP = 4
TOP_K = 2
D_EXPERT = 1024

LANES = 128
SUBLANES = 8
ROW_TILE = 256
MOE_BLOCK = 256
VMEM_LIMIT = 56 * 1024 * 1024
HIGHEST = lax.Precision.HIGHEST


def _params(*sem, **kw):
    return pltpu.CompilerParams(dimension_semantics=sem, vmem_limit_bytes=VMEM_LIMIT, **kw)


def _sigmoid(x):
    return 1.0 / (1.0 + jnp.exp(-x))


def _tiles(B, L):
    if L >= ROW_TILE:
        assert L % ROW_TILE == 0
        return 1, ROW_TILE
    assert ROW_TILE % L == 0 and L % SUBLANES == 0 and B % (ROW_TILE // L) == 0
    return ROW_TILE // L, L


def _mod_spec(bb, l, k):
    return pl.BlockSpec((None, None, bb, 1, D_MODEL), lambda b, i, *_: (l, k, b, 0, 0))


def _ada_kernel(c_ref, w_ref, b_ref, o_ref):
    c = c_ref[...]
    s = (c * _sigmoid(c)).astype(BF16)
    o_ref[0] = jnp.dot(s, w_ref[0].astype(BF16), preferred_element_type=F32) + b_ref[0]


def _ada(c_all, w_ada, b_ada):
    bc = c_all.shape[0]
    tn = 512
    return pl.pallas_call(
        _ada_kernel,
        grid=(DEPTH, 6 * D_MODEL // tn),
        in_specs=[pl.BlockSpec((bc, D_MODEL), lambda l, n: (0, 0)),
                  pl.BlockSpec((1, D_MODEL, tn), lambda l, n: (l, 0, n)),
                  pl.BlockSpec((1, 1, tn), lambda l, n: (l, 0, n))],
        out_specs=pl.BlockSpec((1, bc, tn), lambda l, n: (l, 0, n)),
        out_shape=jax.ShapeDtypeStruct((DEPTH, bc, 6 * D_MODEL), F32),
        compiler_params=_params("arbitrary", "arbitrary"),
        name="ada_mod",
    )(c_all, w_ada, b_ada.reshape(DEPTH, 1, 6 * D_MODEL))


def _ada_norm_val(x, g, shift, scale):
    ms = jnp.mean(x * x, axis=-1, keepdims=True)
    return x * lax.rsqrt(ms + RMS_EPS) * g * (1.0 + scale) + shift


def _inproj_kernel(x_ref, g_ref, sh_ref, sc_ref, wp_ref, wz_ref, p_ref, z_ref):
    bb, tl, _ = x_ref.shape
    h = _ada_norm_val(x_ref[...], g_ref[...], sh_ref[...], sc_ref[...])
    h = h.reshape(bb * tl, D_MODEL).astype(BF16)
    p_ref[...] = jnp.dot(h, wp_ref[...], preferred_element_type=F32).reshape(bb, tl, POOL_WIDTH)
    step = 1152
    for n0 in range(0, Z_WIDTH, step):
        z_ref[:, :, n0:n0 + step] = jnp.dot(
            h, wz_ref[:, n0:n0 + step], preferred_element_type=F32).reshape(bb, tl, step)


def _inproj(x, g, mod, l, wp, wz):
    B, L, _ = x.shape
    bb, tl = _tiles(B, L)
    once = pl.Buffered(1)
    return pl.pallas_call(
        _inproj_kernel,
        grid=(B // bb, L // tl),
        in_specs=[pl.BlockSpec((bb, tl, D_MODEL), lambda b, i: (b, i, 0)),
                  pl.BlockSpec((1, D_MODEL), lambda b, i: (0, 0)),
                  _mod_spec(bb, l, 0), _mod_spec(bb, l, 1),
                  pl.BlockSpec((D_MODEL, POOL_WIDTH), lambda b, i: (0, 0), pipeline_mode=once),
                  pl.BlockSpec((D_MODEL, Z_WIDTH), lambda b, i: (0, 0), pipeline_mode=once)],
        out_specs=[pl.BlockSpec((bb, tl, POOL_WIDTH), lambda b, i: (b, i, 0)),
                   pl.BlockSpec((bb, tl, Z_WIDTH), lambda b, i: (b, i, 0))],
        out_shape=[jax.ShapeDtypeStruct((B, L, POOL_WIDTH), F32),
                   jax.ShapeDtypeStruct((B, L, Z_WIDTH), F32)],
        compiler_params=_params("arbitrary", "arbitrary"),
        name="norm_inproj",
    )(x, g, mod, mod, wp, wz)


def _pool_kernel(pos0, p_ref, halo_ref, st_ref, w_ref, scale_ref, y_ref, buf_ref, ext_ref):
    bb, tl, _ = p_ref.shape
    i = pl.program_id(1)

    @pl.when(i == 0)
    def _():
        ext_ref[:, 0:POOL_HALO, :] = st_ref[...]

    @pl.when(i > 0)
    def _():
        ext_ref[:, 0:POOL_HALO, :] = halo_ref[...]

    ext_ref[:, POOL_HALO:POOL_HALO + tl, :] = p_ref[...]
    pos = pos0 + i * tl + lax.broadcasted_iota(jnp.int32, (1, tl, POOL_GROUP), 1)
    for gi, win in enumerate(POOL_WINDOWS):
        c0 = gi * POOL_GROUP
        cur = p_ref[:, :, c0:c0 + POOL_GROUP]
        acc = cur
        for s in range(1, win):
            acc = acc + ext_ref[:, POOL_HALO - s:POOL_HALO - s + tl, c0:c0 + POOL_GROUP]
        cnt = jnp.minimum(pos + 1, win).astype(F32)
        pooled = (acc / cnt - cur).reshape(bb * tl, POOL_GROUP).astype(BF16)
        out = jnp.dot(pooled, w_ref[gi], preferred_element_type=F32) * scale_ref[:, c0:c0 + POOL_GROUP]
        y_ref[:, c0:c0 + POOL_GROUP] = out.astype(BF16)

    @pl.when(i == pl.num_programs(1) - 1)
    def _():
        buf_ref[...] = ext_ref[:, tl + 1:tl + POOL_HALO, :]


def _pool(p, st16, w_pool, scale, pos0):
    B, L, _ = p.shape
    bb, tl = _tiles(B, L)
    nl = L // tl
    if L >= POOL_HALO:
        halo, hstep = p, tl // POOL_HALO
        halo_spec = pl.BlockSpec((bb, POOL_HALO, POOL_WIDTH),
                                 lambda b, i: (b, jnp.maximum(i * hstep - 1, 0), 0))
    else:
        halo = st16
        halo_spec = pl.BlockSpec((bb, POOL_HALO, POOL_WIDTH), lambda b, i: (b, 0, 0))
    return pl.pallas_call(
        functools.partial(_pool_kernel, pos0),
        grid=(B // bb, nl),
        in_specs=[pl.BlockSpec((bb, tl, POOL_WIDTH), lambda b, i: (b, i, 0)),
                  halo_spec,
                  pl.BlockSpec((bb, POOL_HALO, POOL_WIDTH), lambda b, i: (b, 0, 0)),
                  pl.BlockSpec((len(POOL_WINDOWS), POOL_GROUP, POOL_GROUP), lambda b, i: (0, 0, 0)),
                  pl.BlockSpec((1, POOL_WIDTH), lambda b, i: (0, 0))],
        out_specs=[pl.BlockSpec((bb * tl, POOL_WIDTH), lambda b, i: (b * nl + i, 0)),
                   pl.BlockSpec((bb, POOL_BUF, POOL_WIDTH), lambda b, i: (b, 0, 0))],
        out_shape=[jax.ShapeDtypeStruct((B * L, POOL_WIDTH), BF16),
                   jax.ShapeDtypeStruct((B, POOL_BUF, POOL_WIDTH), F32)],
        scratch_shapes=[pltpu.VMEM((bb, POOL_HALO + tl, POOL_WIDTH), F32)],
        compiler_params=_params("arbitrary", "arbitrary"),
        name="pool_mixer",
    )(p, halo, st16, w_pool, scale)


def _seg_mats():
    ch = np.arange(RWKV_WIDTH) // HEAD_DIM
    ln = np.arange(LANES)
    seg = (ch[:, None] == (ln[None, :] % N_HEADS)).astype(np.float32)
    exp = ((ln[:, None] == ch[None, :])).astype(np.float32)
    stack3 = lambda m: jnp.asarray(np.concatenate([m, m, m], axis=0), dtype=BF16)
    return stack3(seg), stack3(exp)


def _pre_kernel(z_ref, zp_ref, st_ref, mu_ref, w0_ref, w2_ref, a0_ref, a2_ref, g2_ref,
                kk_ref, ka_ref, rk_ref, seg_ref, exp_ref,
                aq_o, wb_o, kk_o, v_o, br_o, kr_o, bs_o, g_o):
    bb, tl, _ = z_ref.shape
    i = pl.program_id(1)
    z = z_ref[...]
    first = jnp.where(i == 0, st_ref[...], zp_ref[:, SUBLANES - 1:SUBLANES, :])
    tpos = lax.broadcasted_iota(jnp.int32, (1, tl, LANES), 1)
    n = bb * tl

    def shifted(c0, c1):
        zc = z[:, :, c0:c1]
        prev = pltpu.roll(zc, 1, axis=1)
        reps = (c1 - c0) // LANES
        t0 = jnp.concatenate([tpos] * reps, axis=2) if reps > 1 else tpos
        prev = jnp.where(t0 == 0, first[:, :, c0:c1], prev)
        return (zc + (prev - zc) * mu_ref[:, c0:c1]).reshape(n, c1 - c0)

    W = RWKV_WIDTH
    r = shifted(0, W)
    k = shifted(W, 2 * W)
    v = shifted(2 * W, 3 * W)
    lora = shifted(LORA_OFF, LORA_OFF + LANES)
    gd = shifted(GATE_OFF, Z_WIDTH)

    w_lin = w0_ref[...] + jnp.dot(jnp.tanh(lora).astype(BF16), w2_ref[...], preferred_element_type=F32)
    a_lin = a0_ref[...] + jnp.dot(lora.astype(BF16), a2_ref[...], preferred_element_type=F32)
    g_o[...] = jnp.dot(_sigmoid(gd).astype(BF16), g2_ref[...], preferred_element_type=F32)
    w = -(jnp.maximum(-w_lin, 0.0) + jnp.log(1.0 + jnp.exp(-jnp.abs(w_lin)))) - 0.5
    decay = jnp.exp(-jnp.exp(w))
    a_sig = _sigmoid(a_lin)

    def select_dot(x, sel3_ref):
        hi = x.astype(BF16)
        r1 = x - hi.astype(F32)
        mid = r1.astype(BF16)
        lo = (r1 - mid.astype(F32)).astype(BF16)
        return jnp.dot(jnp.concatenate([hi, mid, lo], axis=1), sel3_ref[...], preferred_element_type=F32)

    def head_sum(x):
        return select_dot(x, seg_ref)

    kk = k * kk_ref[...]
    inv = 1.0 / jnp.maximum(jnp.sqrt(head_sum(kk * kk)), 1e-12)
    kk = kk * select_dot(inv, exp_ref)
    k2 = k * (1.0 + (a_sig - 1.0) * ka_ref[...])
    b = kk * a_sig

    low_half = lax.broadcasted_iota(jnp.int32, (n, LANES), 1) < HEAD_DIM

    def store_pair(o_ref, xa, xb):
        for c in range(RWKV_WIDTH // LANES):
            cols = slice(c * LANES, (c + 1) * LANES)
            ac, bc = xa[:, cols], xb[:, cols]
            a_sw = pltpu.roll(ac, HEAD_DIM, axis=1)
            b_sw = pltpu.roll(bc, HEAD_DIM, axis=1)
            o_ref[pl.ds(2 * c, n, stride=N_HEADS), :] = jnp.where(low_half, ac, b_sw)
            o_ref[pl.ds(2 * c + 1, n, stride=N_HEADS), :] = jnp.where(low_half, a_sw, bc)

    store_pair(aq_o, -kk, decay * r)
    store_pair(wb_o, decay, b)
    store_pair(kk_o, k2, k2)
    for c in range(RWKV_WIDTH // LANES):
        v_o[pl.ds(c, n, stride=SUBLANES), :] = v[:, c * LANES:(c + 1) * LANES]
    br_o[...] = head_sum(b * r)
    kr_o[...] = head_sum(k2 * r)
    bs_o[...] = head_sum(r * k2 * rk_ref[...])


def _rwkv_pre(z, st_shift, lw):
    B, L, _ = z.shape
    bb, tl = _tiles(B, L)
    nl = L // tl
    pstep = tl // SUBLANES
    seg, exp = _seg_mats()
    row = lambda wdt: pl.BlockSpec((1, wdt), lambda b, i: (0, 0))
    full = lambda a: pl.BlockSpec(a.shape, lambda b, i: (0,) * a.ndim)
    out_w = pl.BlockSpec((bb * tl, RWKV_WIDTH), lambda b, i: (b * nl + i, 0))
    out_s = pl.BlockSpec((bb * tl, LANES), lambda b, i: (b * nl + i, 0))
    out_p = pl.BlockSpec((bb * tl * N_HEADS, LANES), lambda b, i: (b * nl + i, 0))
    out_v = pl.BlockSpec((bb * tl * SUBLANES, LANES), lambda b, i: (b * nl + i, 0))
    T = B * L
    wide = jax.ShapeDtypeStruct((T, RWKV_WIDTH), F32)
    pair = jax.ShapeDtypeStruct((T * N_HEADS, LANES), F32)
    folded = jax.ShapeDtypeStruct((T * SUBLANES, LANES), F32)
    small = jax.ShapeDtypeStruct((T, LANES), F32)
    return pl.pallas_call(
        _pre_kernel,
        grid=(B // bb, nl),
        in_specs=[pl.BlockSpec((bb, tl, Z_WIDTH), lambda b, i: (b, i, 0)),
                  pl.BlockSpec((bb, SUBLANES, Z_WIDTH), lambda b, i: (b, jnp.maximum(i * pstep - 1, 0), 0)),
                  pl.BlockSpec((bb, 1, Z_WIDTH), lambda b, i: (b, 0, 0)),
                  row(Z_WIDTH), row(RWKV_WIDTH), full(lw["w2"]), row(RWKV_WIDTH), full(lw["a2"]),
                  full(lw["g2"]), row(RWKV_WIDTH), row(RWKV_WIDTH), row(RWKV_WIDTH), full(seg), full(exp)],
        out_specs=[out_p] * 3 + [out_v] + [out_s] * 3 + [out_w],
        out_shape=[pair] * 3 + [folded] + [small] * 3 + [wide],
        compiler_params=_params("arbitrary", "arbitrary"),
        name="rwkv_pre",
    )(z, z, st_shift, lw["mu"], lw["w0"], lw["w2"], lw["a0"], lw["a2"], lw["g2"],
      lw["k_k"], lw["k_a"], lw["r_k"], seg, exp)


N_ACC = 4


def _scan_kernel(aq_ref, wb_ref, kk_ref, v_ref, br_ref, kr_ref, s0_ref, y_ref, s_ref, op_tiles, op_first):
    tc = v_ref.shape[0]
    rows = SUBLANES * N_HEADS
    n_groups = tc // SUBLANES

    @pl.when(pl.program_id(1) == 0)
    def _():
        s_ref[...] = s0_ref[...]

    def keys_to_sublanes(blk):
        return jnp.concatenate([blk] * SUBLANES, axis=0).T

    def put_pair(op_ref, slot, first, tile):
        op_ref[slot, first] = tile[0:HEAD_DIM]
        op_ref[slot, first + 1] = tile[HEAD_DIM:]

    def tree_sum(parts):
        while len(parts) > 1:
            parts = [parts[i] + parts[i + 1] for i in range(0, len(parts), 2)]
        return parts[0]

    def reduce_keys(op_ref, slot, state_of):
        sa_p = [None] * N_ACC
        yq_p = [None] * N_ACC
        for j in range(HEAD_DIM):
            sj = state_of(j)
            pa = sj * op_ref[slot, 0, pl.ds(j, 1), :]
            pq = sj * op_ref[slot, 1, pl.ds(j, 1), :]
            i = j % N_ACC
            sa_p[i] = pa if sa_p[i] is None else sa_p[i] + pa
            yq_p[i] = pq if yq_p[i] is None else yq_p[i] + pq
        return tree_sum(sa_p), tree_sum(yq_p)

    def prepare(g, op_ref):
        r0 = pl.multiple_of(jnp.minimum(g, n_groups - 1) * rows, rows)
        r_next = pl.multiple_of(jnp.minimum(g + 1, n_groups - 1) * rows, rows)
        aq = aq_ref[pl.ds(r0, rows), :]
        wb = wb_ref[pl.ds(r0, rows), :]
        kx = kk_ref[pl.ds(r0, rows), :]
        aq_next = aq_ref[pl.ds(r_next, N_HEADS), :]
        for t in range(SUBLANES):
            nxt = aq[(t + 1) * N_HEADS:(t + 2) * N_HEADS] if t + 1 < SUBLANES else aq_next
            put_pair(op_ref, t, 0, keys_to_sublanes(nxt))
            put_pair(op_ref, t, 2, keys_to_sublanes(wb[t * N_HEADS:(t + 1) * N_HEADS]))
            op_ref[t, 4] = keys_to_sublanes(kx[t * N_HEADS:(t + 1) * N_HEADS])[0:HEAD_DIM]

    def run_group(g, carry, op_ref):
        sa, yq = carry
        for t in range(SUBLANES):
            step = g * SUBLANES + t
            v = v_ref[step]
            y_ref[step] = yq + sa * br_ref[pl.ds(step, 1), :] + v * kr_ref[pl.ds(step, 1), :]

            def updated(j, t=t, sa=sa, v=v):
                sj = (s_ref[0, j] * op_ref[t, 2, pl.ds(j, 1), :] + sa * op_ref[t, 3, pl.ds(j, 1), :]
                      + v * op_ref[t, 4, pl.ds(j, 1), :])
                s_ref[0, j] = sj
                return sj

            sa, yq = reduce_keys(op_ref, t, updated)
        return sa, yq

    put_pair(op_first, 0, 0, keys_to_sublanes(aq_ref[0:N_HEADS, :]))
    first = reduce_keys(op_first, 0, lambda j: s_ref[0, j])

    def one_group(g, carry):
        prepare(g, op_tiles)
        return run_group(g, carry, op_tiles)

    lax.fori_loop(0, n_groups, one_group, first)


def _scan(ops, s0, B, L):
    aq, wb, kk, v, br, kr = ops
    tc = min(L, 256)
    nc = L // tc
    T = B * L
    nblk = RWKV_WIDTH // LANES
    op_spec = pl.BlockSpec((tc * N_HEADS, LANES), lambda bi, c: (bi * nc + c, 0))
    v_spec = pl.BlockSpec((tc, nblk, LANES), lambda bi, c: (bi * nc + c, 0, 0))
    s_spec = pl.BlockSpec((tc, LANES), lambda bi, c: (bi * nc + c, 0))
    st_spec = pl.BlockSpec((1, HEAD_DIM, nblk, LANES), lambda bi, c: (bi, 0, 0, 0))
    y, s = pl.pallas_call(
        _scan_kernel,
        grid=(B, nc),
        in_specs=[op_spec] * 3 + [v_spec, s_spec, s_spec, st_spec],
        out_specs=[v_spec, st_spec],
        out_shape=[jax.ShapeDtypeStruct((T, nblk, LANES), F32),
                   jax.ShapeDtypeStruct((B, HEAD_DIM, nblk, LANES), F32)],
        scratch_shapes=[pltpu.VMEM((SUBLANES, 5, HEAD_DIM, LANES), F32),
                        pltpu.VMEM((1, 2, HEAD_DIM, LANES), F32)],
        compiler_params=_params("arbitrary", "arbitrary"),
        name="rwkv_scan",
    )(aq, wb, kk, v.reshape(T, nblk, LANES), br, kr, s0)
    return y.reshape(T * nblk, LANES), s


def _post_kernel(y_ref, v_ref, g_ref, bs_ref, yp_ref, x_ref, lw_ref, lb_ref, wo_ref,
                 gta_ref, n2_ref, sh_ref, sc_ref, wr_ref, x1_ref, h2_ref, s_ref):
    bb, tl, _ = x_ref.shape
    nblk = RWKV_WIDTH // LANES
    n = bb * tl
    cols = [slice(c * LANES, (c + 1) * LANES) for c in range(nblk)]
    folded = lambda ref, c: ref[pl.ds(c, n, stride=nblk), :]
    y = [folded(y_ref, c) for c in range(nblk)]

    def head_total(parts):
        s = parts[0]
        for part in parts[1:]:
            s = s + part
        for sh in (N_HEADS, 2 * N_HEADS, 4 * N_HEADS):
            s = s + pltpu.roll(s, sh, axis=1)
        return s

    mean = head_total(y) * (1.0 / HEAD_DIM)
    dev = [yc - mean for yc in y]
    var = head_total([d * d for d in dev]) * (1.0 / HEAD_DIM)
    rstd = lax.rsqrt(var + LNX_EPS)
    bs = bs_ref[...]
    outs = []
    for ci, c in enumerate(cols):
        yn = dev[ci] * rstd * lw_ref[:, c] + lb_ref[:, c]
        outs.append(((yn + bs * folded(v_ref, ci)) * g_ref[:, c]).astype(BF16))
    yr = jnp.concatenate(outs, axis=1)
    mix = jnp.dot(yp_ref[...], wo_ref[0:POOL_WIDTH, :], preferred_element_type=F32)
    mix = mix + jnp.dot(yr, wo_ref[POOL_WIDTH:, :], preferred_element_type=F32)
    x1 = x_ref[...] + gta_ref[...] * mix.reshape(bb, tl, D_MODEL)
    x1_ref[...] = x1
    h2 = _ada_norm_val(x1, n2_ref[...], sh_ref[...], sc_ref[...]).reshape(bb * tl, D_MODEL)
    h2_ref[...] = h2
    s_ref[...] = _sigmoid(jnp.dot(h2, wr_ref[...], precision=HIGHEST, preferred_element_type=F32))


def _post(y, v, g, bs, ypool, x, lw, mod, l, w_router_p):
    B, L, _ = x.shape
    bb, tl = _tiles(B, L)
    nl = L // tl
    T = B * L
    rows = lambda wdt: pl.BlockSpec((bb * tl, wdt), lambda b, i: (b * nl + i, 0))
    fold_rows = pl.BlockSpec((bb * tl * RWKV_WIDTH // LANES, LANES), lambda b, i: (b * nl + i, 0))
    row = lambda wdt: pl.BlockSpec((1, wdt), lambda b, i: (0, 0))
    xs = pl.BlockSpec((bb, tl, D_MODEL), lambda b, i: (b, i, 0))
    return pl.pallas_call(
        _post_kernel,
        grid=(B // bb, nl),
        in_specs=[fold_rows, fold_rows, rows(RWKV_WIDTH), rows(LANES), rows(POOL_WIDTH), xs,
                  row(RWKV_WIDTH), row(RWKV_WIDTH),
                  pl.BlockSpec((D_MODEL, D_MODEL), lambda b, i: (0, 0)),
                  _mod_spec(bb, l, 2), row(D_MODEL), _mod_spec(bb, l, 3), _mod_spec(bb, l, 4),
                  pl.BlockSpec((D_MODEL, LANES), lambda b, i: (0, 0))],
        out_specs=[xs, rows(D_MODEL), rows(LANES)],
        out_shape=[jax.ShapeDtypeStruct((B, L, D_MODEL), F32),
                   jax.ShapeDtypeStruct((T, D_MODEL), F32),
                   jax.ShapeDtypeStruct((T, LANES), F32)],
        compiler_params=_params("arbitrary", "arbitrary"),
        name="post_outproj_norm2",
    )(y, v, g, bs, ypool, x, lw["lnx_w"], lw["lnx_b"], lw["w_out"], mod, lw["norm2_g"], mod, mod, w_router_p)


def _route(scores, b_router):
    T = scores.shape[0]
    sel = scores + b_router.astype(F32)
    neg = jnp.float32(-jnp.inf)

    def top2(x):
        lane = lax.broadcasted_iota(jnp.int32, x.shape, x.ndim - 1)
        i0 = jnp.argmax(x, -1)
        m0 = jnp.max(x, -1)
        rest = jnp.where(lane == i0[..., None], neg, x)
        return m0, jnp.max(rest, -1), i0, jnp.argmax(rest, -1)

    g0, g1, _, _ = top2(sel.reshape(T, N_EXPERT_GROUPS, EXPERTS_PER_GROUP))
    gbest = jnp.argmax(g0 + g1, -1)
    in_group = (jnp.arange(N_EXPERTS) // EXPERTS_PER_GROUP)[None, :] == gbest[:, None]
    _, _, i0, i1 = top2(jnp.where(in_group, sel, neg))
    idx = jnp.stack([i0, i1], axis=1).astype(jnp.int32)
    wts = jnp.take_along_axis(scores, idx, 1)
    wts = wts / jnp.sum(wts, -1, keepdims=True)
    return idx, wts


def _moe_rows(T):
    return -(-(T * TOP_K + N_EXPERTS * (MOE_BLOCK - 1)) // MOE_BLOCK)


def _dispatch_plan(idx):
    T = idx.shape[0]
    A = T * TOP_K
    nb = _moe_rows(T)
    e_flat = idx.reshape(A)
    experts = jnp.arange(N_EXPERTS, dtype=jnp.int32)
    onehot = (e_flat[:, None] == experts[None, :]).astype(jnp.int32)
    csum = jnp.cumsum(onehot, axis=0)
    rank = jnp.sum(csum * onehot, axis=1) - 1
    counts = csum[-1]
    padded = (counts + MOE_BLOCK - 1) // MOE_BLOCK * MOE_BLOCK
    pend = jnp.cumsum(padded)
    pstart = pend - padded
    dest = (jnp.sum(onehot * pstart[None, :], axis=1) + rank).astype(jnp.int32)
    blk0 = jnp.arange(nb, dtype=jnp.int32) * MOE_BLOCK
    block_e = jnp.sum((pend[None, :] <= blk0[:, None]).astype(jnp.int32), axis=1)
    block_e = jnp.minimum(block_e, N_EXPERTS - 1).astype(jnp.int32)
    n_valid = (pend[-1] // MOE_BLOCK).astype(jnp.int32).reshape(1)
    pad_start = jnp.concatenate([pstart + counts, pend[-1:]]).astype(jnp.int32)
    pad_cnt = jnp.concatenate([padded - counts, nb * MOE_BLOCK - pend[-1:]]).astype(jnp.int32)
    return dest, block_e, n_valid, pad_start, pad_cnt


def _dispatch_kernel(tiles_a, dest_ref, pad_start_ref, pad_cnt_ref, ha_ref, hb_ref, xs_ref, sem):
    i = pl.program_id(0)
    base = i * (TOP_K * ROW_TILE)

    def row_copy(h_ref, src_row, dst_row):
        return pltpu.make_async_copy(h_ref.at[pl.ds(src_row, 1), :], xs_ref.at[pl.ds(dst_row, 1), :], sem)

    def scatter_tile(h_ref):
        def issue(r, carry):
            for k in range(TOP_K):
                row_copy(h_ref, r, dest_ref[base + TOP_K * r + k]).start(priority=k)
            return carry

        def drain(r, carry):
            for k in range(TOP_K):
                row_copy(h_ref, 0, 0).wait()
            return carry

        lax.fori_loop(0, ROW_TILE, issue, 0, unroll=8)
        lax.fori_loop(0, ROW_TILE, drain, 0, unroll=8)

    @pl.when(i < tiles_a)
    def _():
        scatter_tile(ha_ref)

    @pl.when(i >= tiles_a)
    def _():
        scatter_tile(hb_ref)

    def fill(e, wait):
        start, cnt = pad_start_ref[e], pad_cnt_ref[e]
        go = (lambda cp: cp.wait()) if wait else (lambda cp: cp.start())
        head = jnp.minimum((-start) & (SUBLANES - 1), cnt)

        def chunk(size, at):
            at = pl.multiple_of(at, SUBLANES)
            return pltpu.make_async_copy(ha_ref.at[pl.ds(0, size), :], xs_ref.at[pl.ds(at, size), :], sem)

        def one(j, carry):
            go(row_copy(ha_ref, 0, start + j))
            return carry

        lax.fori_loop(0, head, one, 0)
        pos = start + head
        rem = cnt - head

        def full(j, carry):
            go(chunk(ROW_TILE, pos + j * ROW_TILE))
            return carry

        n_full = rem // ROW_TILE
        lax.fori_loop(0, n_full, full, 0)
        pos = pos + n_full * ROW_TILE
        size = ROW_TILE // 2
        while size >= SUBLANES:
            take = (rem & size) != 0

            @pl.when(take)
            def _(size=size, pos=pos):
                go(chunk(size, pos))

            pos = pos + jnp.where(take, size, 0)
            size //= 2

    @pl.when(i == 0)
    def _():
        for e in range(N_EXPERTS + 1):
            fill(e, False)
        for e in range(N_EXPERTS + 1):
            fill(e, True)


def _dispatch_rows(ha, hb, dest, pad_start, pad_cnt, n_rows):
    tiles_a, tiles_b = ha.shape[0] // ROW_TILE, hb.shape[0] // ROW_TILE
    return pl.pallas_call(
        functools.partial(_dispatch_kernel, tiles_a),
        grid_spec=pltpu.PrefetchScalarGridSpec(
            num_scalar_prefetch=3,
            grid=(tiles_a + tiles_b,),
            in_specs=[pl.BlockSpec((ROW_TILE, D_MODEL), lambda i, *_: (jnp.minimum(i, tiles_a - 1), 0)),
                      pl.BlockSpec((ROW_TILE, D_MODEL), lambda i, *_: (jnp.maximum(i - tiles_a, 0), 0))],
            out_specs=pl.BlockSpec(memory_space=pl.ANY),
            scratch_shapes=[pltpu.SemaphoreType.DMA(())]),
        out_shape=jax.ShapeDtypeStruct((n_rows, D_MODEL), F32),
        compiler_params=_params("arbitrary"),
        name="moe_dispatch",
    )(dest, pad_start, pad_cnt, ha, hb)


def _expert_changed(be_ref, b):
    return jnp.logical_or(b == 0, be_ref[b] != be_ref[jnp.maximum(b - 1, 0)])


def _gateup_kernel(be_ref, nv_ref, x_ref, wg_ref, wu_ref, o_ref, wgc, wuc):
    b = pl.program_id(1)

    @pl.when(_expert_changed(be_ref, b))
    def _():
        wgc[...] = wg_ref[0, 0].astype(BF16)
        wuc[...] = wu_ref[0, 0].astype(BF16)

    @pl.when(b < nv_ref[0])
    def _():
        x = x_ref[...].astype(BF16)
        gate = jnp.dot(x, wgc[...], preferred_element_type=F32)
        up = jnp.dot(x, wuc[...], preferred_element_type=F32)
        o_ref[...] = (gate * _sigmoid(gate) * up).astype(BF16)

    @pl.when(b >= nv_ref[0])
    def _():
        o_ref[...] = jnp.zeros_like(o_ref)


def _down_kernel(be_ref, nv_ref, a_ref, wd_ref, o_ref, wdc):
    b = pl.program_id(1)

    @pl.when(_expert_changed(be_ref, b))
    def _():
        wdc[...] = wd_ref[0, 0].astype(BF16)

    @pl.when(b < nv_ref[0])
    def _():
        o_ref[...] = jnp.dot(a_ref[...], wdc[...], preferred_element_type=F32)

    @pl.when(b >= nv_ref[0])
    def _():
        o_ref[...] = jnp.zeros_like(o_ref)


def _moe_matmuls(l, xs, block_e, n_valid, w_gate, w_up, w_down):
    R = xs.shape[0]
    nb = R // MOE_BLOCK
    ce = D_EXPERT
    live = lambda b, nv: jnp.minimum(b, nv[0] - 1)
    act = pl.pallas_call(
        _gateup_kernel,
        grid_spec=pltpu.PrefetchScalarGridSpec(
            num_scalar_prefetch=2,
            grid=(D_EXPERT // ce, nb),
            in_specs=[pl.BlockSpec((MOE_BLOCK, D_MODEL), lambda c, b, be, nv: (live(b, nv), 0)),
                      pl.BlockSpec((1, 1, D_MODEL, ce), lambda c, b, be, nv: (l, be[b], 0, c)),
                      pl.BlockSpec((1, 1, D_MODEL, ce), lambda c, b, be, nv: (l, be[b], 0, c))],
            out_specs=pl.BlockSpec((MOE_BLOCK, ce), lambda c, b, be, nv: (b, c)),
            scratch_shapes=[pltpu.VMEM((D_MODEL, ce), BF16), pltpu.VMEM((D_MODEL, ce), BF16)]),
        out_shape=jax.ShapeDtypeStruct((R, D_EXPERT), BF16),
        compiler_params=_params("arbitrary", "arbitrary"),
        name="moe_gate_up",
    )(block_e, n_valid, xs, w_gate, w_up)
    cd = D_MODEL
    return pl.pallas_call(
        _down_kernel,
        grid_spec=pltpu.PrefetchScalarGridSpec(
            num_scalar_prefetch=2,
            grid=(D_MODEL // cd, nb),
            in_specs=[pl.BlockSpec((MOE_BLOCK, D_EXPERT), lambda c, b, be, nv: (live(b, nv), 0)),
                      pl.BlockSpec((1, 1, D_EXPERT, cd), lambda c, b, be, nv: (l, be[b], 0, c))],
            out_specs=pl.BlockSpec((MOE_BLOCK, cd), lambda c, b, be, nv: (b, c)),
            scratch_shapes=[pltpu.VMEM((D_EXPERT, cd), BF16)]),
        out_shape=jax.ShapeDtypeStruct((R, D_MODEL), F32),
        compiler_params=_params("arbitrary", "arbitrary"),
        name="moe_down",
    )(block_e, n_valid, act, w_down)


def _combine_kernel(final, pos_ref, x_ref, w0_ref, w1_ref, gt_ref, fg_ref, yb_ref, o_ref, y0_buf, y1_buf, sem):
    bb, tl, _ = x_ref.shape
    tile = pl.program_id(0) * pl.num_programs(1) + pl.program_id(1)
    base = tile * (TOP_K * ROW_TILE)

    def row_copy(src_row, buf, r):
        return pltpu.make_async_copy(yb_ref.at[pl.ds(src_row, 1), :], buf.at[pl.ds(r, 1), :], sem)

    def issue(r, carry):
        row_copy(pos_ref[base + TOP_K * r], y0_buf, r).start(priority=0)
        row_copy(pos_ref[base + TOP_K * r + 1], y1_buf, r).start(priority=1)
        return carry

    def drain(r, carry):
        row_copy(0, y0_buf, 0).wait()
        row_copy(0, y1_buf, 0).wait()
        return carry

    lax.fori_loop(0, ROW_TILE, issue, 0, unroll=8)
    lax.fori_loop(0, ROW_TILE, drain, 0, unroll=8)

    w0 = w0_ref[...]
    w1 = w1_ref[...]
    for c in range(D_MODEL // LANES):
        cols = slice(c * LANES, (c + 1) * LANES)
        ff = y0_buf[:, cols] * w0 + y1_buf[:, cols] * w1
        o_ref[:, :, cols] = x_ref[:, :, cols] + gt_ref[:, :, cols] * ff.reshape(bb, tl, LANES)
    if final:
        x2 = o_ref[...]
        ms = jnp.mean(x2 * x2, axis=-1, keepdims=True)
        o_ref[...] = x2 * lax.rsqrt(ms + RMS_EPS) * fg_ref[...]


def _combine(x1, yb, pos, w0b, w1b, mod, l, final_g, final):
    B, L, _ = x1.shape
    bb, tl = _tiles(B, L)
    nl = L // tl
    xs = pl.BlockSpec((bb, tl, D_MODEL), lambda b, i, *_: (b, i, 0))
    rows = pl.BlockSpec((bb * tl, LANES), lambda b, i, *_: (b * nl + i, 0))
    return pl.pallas_call(
        functools.partial(_combine_kernel, final),
        grid_spec=pltpu.PrefetchScalarGridSpec(
            num_scalar_prefetch=1,
            grid=(B // bb, nl),
            in_specs=[xs, rows, rows, _mod_spec(bb, l, 5),
                      pl.BlockSpec((1, D_MODEL), lambda b, i, *_: (0, 0)),
                      pl.BlockSpec(memory_space=pl.ANY)],
            out_specs=xs,
            scratch_shapes=[pltpu.VMEM((ROW_TILE, D_MODEL), F32), pltpu.VMEM((ROW_TILE, D_MODEL), F32),
                            pltpu.SemaphoreType.DMA(())]),
        out_shape=jax.ShapeDtypeStruct((B, L, D_MODEL), F32),
        compiler_params=_params("arbitrary", "arbitrary"),
        name="moe_combine",
    )(pos, x1, w0b, w1b, mod, final_g, yb)


def _vh(x, axis=-1):
    axis = axis % x.ndim
    split = x.reshape(x.shape[:axis] + (N_HEADS, HEAD_DIM) + x.shape[axis + 1:])
    return jnp.swapaxes(split, axis, axis + 1).reshape(x.shape)


def _vh_inverse(x, axis=-1):
    axis = axis % x.ndim
    split = x.reshape(x.shape[:axis] + (HEAD_DIM, N_HEADS) + x.shape[axis + 1:])
    return jnp.swapaxes(split, axis, axis + 1).reshape(x.shape)


def _layer_weights(l, w_in, w_pool, pool_scale, mu_shift, w0, w2, a0, a2, g2, k_k, k_a, r_k,
                   lnx_w, lnx_b, w_out, norm1_g, norm2_g):
    W = RWKV_WIDTH
    wz = w_in[l][:, POOL_WIDTH:]
    wz = jnp.concatenate([wz[:, :2 * W], _vh(wz[:, 2 * W:3 * W]), wz[:, 3 * W:],
                          jnp.zeros((D_MODEL, Z_WIDTH - SHIFT_WIDTH), F32)], axis=1).astype(BF16)
    mu = mu_shift[l]
    mu = jnp.concatenate([mu[:2 * W], _vh(mu[2 * W:3 * W]), mu[3 * W:],
                          jnp.zeros((Z_WIDTH - SHIFT_WIDTH,), F32)]).reshape(1, Z_WIDTH)
    zpad = lambda m, top, tot: jnp.concatenate(
        [jnp.zeros((top, W), F32), m, jnp.zeros((tot - top - m.shape[0], W), F32)], axis=0).astype(BF16)
    wo = w_out[l]
    wo = jnp.concatenate([wo[:POOL_WIDTH], _vh(wo[POOL_WIDTH:], axis=0)], axis=0).astype(BF16)
    r1 = lambda x: x.reshape(1, -1)
    return dict(
        wp=w_in[l][:, :POOL_WIDTH].astype(BF16), wz=wz, mu=mu,
        w_pool=w_pool[l].astype(BF16), pool_scale=r1(pool_scale[l]),
        w0=r1(w0[l]), w2=zpad(w2[l], 0, LANES), a0=r1(a0[l]), a2=zpad(a2[l], DECAY_LORA, LANES),
        g2=zpad(_vh(g2[l]), 0, Z_WIDTH - GATE_OFF),
        k_k=r1(k_k[l]), k_a=r1(k_a[l]), r_k=r1(r_k[l]),
        lnx_w=r1(_vh(lnx_w[l])), lnx_b=r1(_vh(lnx_b[l])), w_out=wo,
        norm1_g=r1(norm1_g[l]), norm2_g=r1(norm2_g[l]))


def _shift_to_z(s):
    W = RWKV_WIDTH
    out = jnp.concatenate([s[:, :2 * W], _vh(s[:, 2 * W:3 * W]), s[:, 3 * W:],
                           jnp.zeros((s.shape[0], Z_WIDTH - SHIFT_WIDTH), s.dtype)], axis=1)
    return out[:, None, :]


def _z_to_shift(zrow):
    W = RWKV_WIDTH
    v = _vh_inverse(zrow[:, 2 * W:3 * W])
    return jnp.concatenate([zrow[:, :2 * W], v, zrow[:, 3 * W:SHIFT_WIDTH]], axis=1)


def _state_to_tile(s):
    B = s.shape[0]
    return s.transpose(0, 3, 2, 1).reshape(B, HEAD_DIM, RWKV_WIDTH // LANES, LANES)


def _tile_to_state(p):
    B = p.shape[0]
    return p.reshape(B, HEAD_DIM, HEAD_DIM, N_HEADS).transpose(0, 3, 2, 1)


def _mix_half(l, x, mod, st_wkv, st_pool, st_shift, pos0, lw, w_router_p):
    B, L, _ = x.shape
    p, z = _inproj(x, lw["norm1_g"], mod, l, lw["wp"], lw["wz"])
    st16 = jnp.concatenate([jnp.zeros((B, 1, POOL_WIDTH), F32), st_pool], axis=1)
    ypool, buf = _pool(p, st16, lw["w_pool"], lw["pool_scale"], pos0)
    aq, wb, kk, v, br, kr, bs, g = _rwkv_pre(z, _shift_to_z(st_shift), lw)
    y, s_fin = _scan((aq, wb, kk, v, br, kr), _state_to_tile(st_wkv), B, L)
    x1, h2, scores = _post(y, v, g, bs, ypool, x, lw, mod, l, w_router_p)
    return x1, h2, scores, _tile_to_state(s_fin), buf, _z_to_shift(z[:, -1])


def kernel(x_prompt, x_sample, state_wkv, state_pool, state_shift, c_prompt, c_sample, w_ada, b_ada, norm1_g, norm2_g, w_in, w_pool, pool_scale, mu_shift, w0, w2, a0, a2, g2, k_k, k_a, r_k, lnx_w, lnx_b, w_out, w_router, b_router, w_gate, w_up, w_down, final_g):
    Bp, Lp, _ = x_prompt.shape
    Bs, Ls, _ = x_sample.shape
    Tp, Ts = Bp * Lp, Bs * Ls
    mod = _ada(jnp.concatenate([c_prompt, c_sample], axis=0), w_ada, b_ada)
    mod = mod.reshape(DEPTH, Bp + Bs, 6, 1, D_MODEL).transpose(0, 2, 1, 3, 4)
    mod_p, mod_s = mod[:, :, :Bp], mod[:, :, Bp:]
    w_router_p = jnp.concatenate([w_router, jnp.zeros((D_MODEL, LANES - N_EXPERTS), F32)], axis=1)
    fg = final_g.reshape(1, D_MODEL)
    n_rows = _moe_rows(Tp + Ts) * MOE_BLOCK

    xp, xs_ = x_prompt, x_sample
    outs_p, outs_s = [], []
    for l in range(DEPTH):
        lw = _layer_weights(l, w_in, w_pool, pool_scale, mu_shift, w0, w2, a0, a2, g2, k_k, k_a, r_k,
                            lnx_w, lnx_b, w_out, norm1_g, norm2_g)
        zp = lambda *s: jnp.zeros(s, F32)
        x1p, h2p, scp, wkv_p, buf_p, shf_p = _mix_half(
            l, xp, mod_p, zp(Bp, N_HEADS, HEAD_DIM, HEAD_DIM), zp(Bp, POOL_BUF, POOL_WIDTH),
            zp(Bp, SHIFT_WIDTH), 0, lw, w_router_p)
        x1s, h2s, scs, wkv_s, buf_s, shf_s = _mix_half(
            l, xs_, mod_s, state_wkv[l], state_pool[l], state_shift[l], PAST_LEN, lw, w_router_p)
        outs_p.append((wkv_p, buf_p, shf_p))
        outs_s.append((wkv_s, buf_s, shf_s))

        scores = jnp.concatenate([scp, scs], axis=0)[:, :N_EXPERTS]
        idx, wts = _route(scores, b_router)
        dest, block_e, n_valid, pad_start, pad_cnt = _dispatch_plan(idx)
        xs_rows = _dispatch_rows(h2p, h2s, dest, pad_start, pad_cnt, n_rows)
        yb = _moe_matmuls(l, xs_rows, block_e, n_valid, w_gate, w_up, w_down)
        w0b = jnp.broadcast_to(wts[:, 0:1], (Tp + Ts, LANES))
        w1b = jnp.broadcast_to(wts[:, 1:2], (Tp + Ts, LANES))
        final = l == DEPTH - 1
        xp = _combine(x1p, yb, dest[:TOP_K * Tp], w0b[:Tp], w1b[:Tp], mod_p, l, fg, final)
        xs_ = _combine(x1s, yb, dest[TOP_K * Tp:], w0b[Tp:], w1b[Tp:], mod_s, l, fg, final)

    st = lambda outs, i: jnp.stack([o[i] for o in outs])
    return (xp, xs_, st(outs_p, 0), st(outs_p, 1), st(outs_p, 2),
            st(outs_s, 0), st(outs_s, 1), st(outs_s, 2))
```
